```python
import math
import jax, jax.numpy as jnp
from jax import lax
import numpy as np

D_MODEL = 1024
BATCH = 32
SEQ = 256
DEPTH = 2
DEC_BATCH = 8
DEC_SEQ = 2048
PAST_LEN = 256

GRID_W = 64
RWKV_HEAD = 64
N_RWKV = 4
D_RWKV = N_RWKV * RWKV_HEAD
LORA_W = 64
LORA_A = 64
GN_EPS = 64e-5
MLA_HEADS = 8
MLA_NOPE = 64
MLA_ROPE = 32
MLA_V = 64
D_MLA = MLA_HEADS * MLA_V
Q_RANK = 256
KV_RANK = 128
ROPE_BASE = 10000.0
ROPE_PAIRS_AXIS = MLA_ROPE // 4
SOFTMAX_SCALE = (MLA_NOPE + MLA_ROPE) ** -0.5
Q_BLOCK = 128
LRU_BLOCKS = 4
LRU_BLOCK = 64
D_LRU = LRU_BLOCKS * LRU_BLOCK
CONV_W = 4
CONV_LEFT = 2
LRU_C = 8.0
IN_SIZES = (D_RWKV, D_RWKV, D_RWKV, D_RWKV, Q_RANK, KV_RANK, MLA_ROPE, D_MLA, D_LRU, D_LRU)
D_IN = sum(IN_SIZES)
D_MIX = D_RWKV + D_MLA + D_LRU
ALPHA = (2 * DEPTH) ** 0.25
OUT_INIT = (8 * DEPTH) ** -0.25
LN_EPS = 1e-5

kernel_name = "hybrid_rwkv7_mla_rglru_diffusion_step"


def layer_norm(x, g=None, b=None):
    xf = x.astype(jnp.float32)
    mu = jnp.mean(xf, -1, keepdims=True)
    var = jnp.mean(jnp.square(xf - mu), -1, keepdims=True)
    y = (xf - mu) * lax.rsqrt(var + LN_EPS)
    if g is not None:
        y = y * g.astype(jnp.float32) + b.astype(jnp.float32)
    return y.astype(x.dtype)


def rms_norm(x, g):
    xf = x.astype(jnp.float32)
    y = xf * lax.rsqrt(jnp.mean(xf * xf, -1, keepdims=True) + 1e-6) * g.astype(jnp.float32)
    return y.astype(x.dtype)


def adaln(cvec, w, b):
    m = jax.nn.silu(cvec) @ w + b
    return jnp.split(m[:, None, :], 3, axis=-1)


def split_in(h):
    idx = [int(v) for v in np.cumsum(IN_SIZES)[:-1]]
    return jnp.split(h, idx, axis=-1)


def axial_rope_tables(L):
    rows = L // GRID_W
    row = jnp.repeat(jnp.arange(rows), GRID_W).astype(jnp.float32)
    col = jnp.tile(jnp.arange(GRID_W), rows).astype(jnp.float32)
    inv = ROPE_BASE ** (-jnp.arange(ROPE_PAIRS_AXIS, dtype=jnp.float32) / ROPE_PAIRS_AXIS)
    ang = jnp.concatenate([row[:, None] * inv, col[:, None] * inv], axis=-1)
    return jnp.cos(ang), jnp.sin(ang)


def apply_rope(x, cos, sin):
    xp = x.astype(jnp.float32).reshape(x.shape[:-1] + (MLA_ROPE // 2, 2))
    x1, x2 = xp[..., 0], xp[..., 1]
    y = jnp.stack([x1 * cos - x2 * sin, x1 * sin + x2 * cos], axis=-1)
    return y.reshape(x.shape).astype(x.dtype)


def rwkv_step(S, inp):
    r_t, w_t, k_t, v_t, kk_t, a_t = inp
    sa = jnp.einsum('ebhvk,ebhk->ebhv', S, -kk_t)
    S = (S * w_t[..., None, :] + sa[..., :, None] * (kk_t * a_t)[..., None, :]
         + v_t[..., :, None] * k_t[..., None, :])
    return S, jnp.einsum('ebhvk,ebhk->ebhv', S, r_t)


def bidir(t):
    return jnp.stack([t, jnp.flip(t, 1)])


def flip_second(t):
    return jnp.stack([t[0], jnp.flip(t[1], 1)])


def rwkv_branch(u, r, k, v, g, S0, l, P):
    B, L, _ = r.shape
    f32 = jnp.float32
    heads = lambda t: t.reshape(t.shape[:-1] + (N_RWKV, RWKV_HEAD))
    wl = P['rw_w0'][l][:, None, None, :] + jnp.einsum(
        'eblr,erc->eblc', jnp.tanh(jnp.einsum('bld,edr->eblr', u, P['rw_w1'][l])), P['rw_w2'][l])
    decay = jnp.exp(-jnp.exp(-jax.nn.softplus(-wl.astype(f32)) - 0.5))
    a = jax.nn.sigmoid((P['rw_a0'][l][:, None, None, :] + jnp.einsum(
        'eblr,erc->eblc', jnp.einsum('bld,edr->eblr', u, P['rw_a1'][l]), P['rw_a2'][l])).astype(f32))
    rf, kf, vf = r.astype(f32), k.astype(f32), v.astype(f32)
    kk = heads(kf * P['rw_kk'][l])
    kk = kk / jnp.maximum(jnp.sqrt(jnp.sum(kk * kk, -1, keepdims=True)), 1e-12)
    k_dir = kf[None] * (1.0 + (a - 1.0) * P['rw_ka'][l])
    seq = (bidir(heads(rf)), flip_second(heads(decay)), flip_second(heads(k_dir)),
           bidir(heads(vf)), bidir(kk), flip_second(heads(a)))
    seq = tuple(jnp.moveaxis(t, 2, 0) for t in seq)
    S_fin, ys = lax.scan(rwkv_step, S0.astype(f32), seq)
    ys = jnp.moveaxis(ys, 0, 2)
    y = ys[0] + jnp.flip(ys[1], 1)
    mu = jnp.mean(y, -1, keepdims=True)
    var = jnp.mean(jnp.square(y - mu), -1, keepdims=True)
    yn = (y - mu) * lax.rsqrt(var + GN_EPS)
    bonus = jnp.sum(heads(rf) * heads(kf) * P['rw_rk'][l], -1, keepdims=True) * heads(vf)
    yn = yn.reshape(B, L, D_RWKV) * P['rw_lnx_g'][l] + P['rw_lnx_b'][l]
    y = (yn + bonus.reshape(B, L, D_RWKV)) * jax.nn.silu(g.astype(f32))
    return y.astype(u.dtype), S_fin


def mla_queries(cq, l, P):
    B, L, _ = cq.shape
    q = (rms_norm(cq, P['mla_qnorm'][l]) @ P['mla_wuq'][l]).reshape(B, L, MLA_HEADS, MLA_NOPE + MLA_ROPE)
    return q[..., :MLA_NOPE], q[..., MLA_NOPE:]


def mla_keyvals(ckv_n, l, P):
    B, L, _ = ckv_n.shape
    kv = (ckv_n @ P['mla_wukv'][l]).reshape(B, L, MLA_HEADS, MLA_NOPE + MLA_V)
    return kv[..., :MLA_NOPE], kv[..., MLA_NOPE:]


def block_attention(qn, qr, kn, kr, v):
    B, Lq = qn.shape[:2]
    nb = Lq // Q_BLOCK

    def one_block(args):
        qn_b, qr_b = args
        s = (jnp.einsum('bqhd,bkhd->bhqk', qn_b, kn)
             + jnp.einsum('bqhr,bkr->bhqk', qr_b, kr)).astype(jnp.float32) * SOFTMAX_SCALE
        p = jax.nn.softmax(s, axis=-1).astype(v.dtype)
        return jnp.einsum('bhqk,bkhd->bqhd', p, v)

    to_blocks = lambda t: jnp.moveaxis(t.reshape((B, nb, Q_BLOCK) + t.shape[2:]), 1, 0)
    out = lax.map(one_block, (to_blocks(qn), to_blocks(qr)))
    return jnp.moveaxis(out, 0, 1).reshape(B, Lq, MLA_HEADS * MLA_V)


def centred_conv(x, w, b):
    L = x.shape[1]
    xp = jnp.pad(x, ((0, 0), (CONV_LEFT, CONV_W - 1 - CONV_LEFT), (0, 0)))
    y = b
    for j in range(CONV_W):
        y = y + xp[:, j:j + L] * w[j]
    return y


def lin_combine(p, q):
    a1, b1 = p
    a2, b2 = q
    return a1 * a2, a2 * b1 + b2


def lru_branch(xl, gl, h0, l, P):
    B, L, _ = xl.shape
    f32 = jnp.float32
    xc = centred_conv(xl, P['lru_conv_w'][l], P['lru_conv_b'][l])
    xblk = xc.reshape(B, L, LRU_BLOCKS, LRU_BLOCK)
    gr = jnp.einsum('blgi,egio->eblgo', xblk, P['lru_wa'][l]).reshape(2, B, L, D_LRU) + P['lru_ba'][l][:, None, None, :]
    gi = jnp.einsum('blgi,egio->eblgo', xblk, P['lru_wx'][l]).reshape(2, B, L, D_LRU) + P['lru_bx'][l][:, None, None, :]
    log_a = -LRU_C * jax.nn.sigmoid(gr.astype(f32)) * jax.nn.softplus(-P['lru_lambda'][l].astype(f32))[:, None, None, :]
    a = jnp.exp(log_a)
    bterm = jnp.sqrt(-jnp.expm1(2.0 * log_a)) * jax.nn.sigmoid(gi.astype(f32)) * xc.astype(f32)[None]
    a = flip_second(a)
    bterm = flip_second(bterm)
    bterm = bterm.at[:, :, 0].add(a[:, :, 0] * h0.astype(f32))
    _, h = lax.associative_scan(lin_combine, (a, bterm), axis=2)
    h_fin = h[:, :, -1]
    y = (h[0] + jnp.flip(h[1], 1)) * jax.nn.silu(gl.astype(f32))
    return y.astype(xl.dtype), h_fin


def trunk_layer(x, cvec, l, P, ctx=None):
    B, L, _ = x.shape
    shift, scale, gate = adaln(cvec, P['w_mod'][l], P['b_mod'][l])
    u = layer_norm(x) * (1.0 + scale) + shift
    r, k, v, g, cq, ckv, kr, gm, xl, gl = split_in(u @ P['w_in'][l])
    ckv_n = rms_norm(ckv, P['mla_kvnorm'][l])
    qn, qr = mla_queries(cq, l, P)
    kn, vv = mla_keyvals(ckv_n, l, P)
    if ctx is None:
        S0 = jnp.zeros((2, B, N_RWKV, RWKV_HEAD, RWKV_HEAD), jnp.float32)
        h0 = jnp.zeros((2, B, D_LRU), jnp.float32)
        kr_all = kr
    else:
        S_ctx, ckv_ctx, kr_ctx, h_ctx = ctx
        S0 = jnp.moveaxis(S_ctx, 1, 0)
        h0 = jnp.moveaxis(h_ctx, 1, 0)
        cos, sin = axial_rope_tables(L)
        qr = apply_rope(qr, cos[:, None, :], sin[:, None, :])
        kn_c, v_c = mla_keyvals(ckv_ctx.astype(x.dtype), l, P)
        kn = jnp.concatenate([kn_c, kn], axis=1)
        vv = jnp.concatenate([v_c, vv], axis=1)
        kr_all = jnp.concatenate([kr_ctx.astype(kr.dtype), apply_rope(kr, cos, sin)], axis=1)
    y_rw, S_fin = rwkv_branch(u, r, k, v, g, S0, l, P)
    y_mla = block_attention(qn, qr, kn, kr_all, vv) * jax.nn.silu(gm)
    y_lru, h_fin = lru_branch(xl, gl, h0, l, P)
    out = jnp.concatenate([y_rw, y_mla, y_lru], axis=-1) @ P['w_out'][l]
    x = layer_norm(ALPHA * x + gate * out, P['ln_g'][l], P['ln_b'][l])
    return x, (jnp.moveaxis(S_fin, 0, 1), ckv_n, kr, jnp.moveaxis(h_fin, 0, 1))


def setup_inputs(seed: int = 0) -> dict:
    key = jax.random.key(seed)
    ks = iter(jax.random.split(key, 48))
    D = D_MODEL

    def nrm(shape, s):
        return s * jax.random.normal(next(ks), shape, jnp.float32)

    a_c = jax.random.uniform(next(ks), (DEPTH, 2, D_LRU), jnp.float32, 0.9, 0.999)
    s_lam = a_c ** (1.0 / LRU_C)
    return {
        'x_prompt': nrm((BATCH, SEQ, D), 1.0),
        'x_sample': nrm((DEC_BATCH, DEC_SEQ, D), 1.0),
        'state_rwkv': nrm((DEC_BATCH, DEPTH, 2, N_RWKV, RWKV_HEAD, RWKV_HEAD), 0.1),
        'cache_mla_ckv': nrm((DEC_BATCH, DEPTH, PAST_LEN, KV_RANK), 1.0),
        'cache_mla_krope': nrm((DEC_BATCH, DEPTH, PAST_LEN, MLA_ROPE), 1.0),
        'state_lru': nrm((DEC_BATCH, DEPTH, 2, D_LRU), 0.5),
        'c': nrm((DEC_BATCH, D), 1.0),
        'c_ctx': nrm((D,), 1.0),
        'w_mod': nrm((DEPTH, D, 3 * D), 0.5 * D ** -0.5),
        'b_mod': nrm((DEPTH, 3 * D), 0.02),
        'w_in': nrm((DEPTH, D, D_IN), D ** -0.5),
        'rw_w0': jax.random.uniform(next(ks), (DEPTH, 2, D_RWKV), jnp.float32, -2.0, 2.0),
        'rw_w1': nrm((DEPTH, 2, D, LORA_W), D ** -0.5),
        'rw_w2': nrm((DEPTH, 2, LORA_W, D_RWKV), 0.1 * LORA_W ** -0.5),
        'rw_a0': nrm((DEPTH, 2, D_RWKV), 0.1),
        'rw_a1': nrm((DEPTH, 2, D, LORA_A), D ** -0.5),
        'rw_a2': nrm((DEPTH, 2, LORA_A, D_RWKV), 0.1 * LORA_A ** -0.5),
        'rw_kk': 0.85 + nrm((DEPTH, D_RWKV), 0.02),
        'rw_ka': 1.0 + nrm((DEPTH, D_RWKV), 0.02),
        'rw_rk': nrm((DEPTH, N_RWKV, RWKV_HEAD), 0.1),
        'rw_lnx_g': 1.0 + nrm((DEPTH, D_RWKV), 0.02),
        'rw_lnx_b': nrm((DEPTH, D_RWKV), 0.02),
        'mla_qnorm': 1.0 + nrm((DEPTH, Q_RANK), 0.02),
        'mla_wuq': nrm((DEPTH, Q_RANK, MLA_HEADS * (MLA_NOPE + MLA_ROPE)), Q_RANK ** -0.5),
        'mla_kvnorm': 1.0 + nrm((DEPTH, KV_RANK), 0.02),
        'mla_wukv': nrm((DEPTH, KV_RANK, MLA_HEADS * (MLA_NOPE + MLA_V)), KV_RANK ** -0.5),
        'lru_conv_w': nrm((DEPTH, CONV_W, D_LRU), CONV_W ** -0.5),
        'lru_conv_b': nrm((DEPTH, D_LRU), 0.02),
        'lru_wa': nrm((DEPTH, 2, LRU_BLOCKS, LRU_BLOCK, LRU_BLOCK), LRU_BLOCK ** -0.5),
        'lru_ba': nrm((DEPTH, 2, D_LRU), 0.02),
        'lru_wx': nrm((DEPTH, 2, LRU_BLOCKS, LRU_BLOCK, LRU_BLOCK), LRU_BLOCK ** -0.5),
        'lru_bx': nrm((DEPTH, 2, D_LRU), 0.02),
        'lru_lambda': jnp.log(s_lam) - jnp.log1p(-s_lam),
        'w_out': nrm((DEPTH, D_MIX, D), OUT_INIT * D_MIX ** -0.5),
        'ln_g': 1.0 + nrm((DEPTH, D), 0.02),
        'ln_b': nrm((DEPTH, D), 0.02),
    }


def reference(x_prompt, x_sample, state_rwkv, cache_mla_ckv, cache_mla_krope, state_lru, c,
              c_ctx, w_mod, b_mod, w_in, rw_w0, rw_w1, rw_w2, rw_a0, rw_a1, rw_a2, rw_kk, rw_ka,
              rw_rk, rw_lnx_g, rw_lnx_b, mla_qnorm, mla_wuq, mla_kvnorm, mla_wukv, lru_conv_w,
              lru_conv_b, lru_wa, lru_ba, lru_wx, lru_bx, lru_lambda, w_out, ln_g, ln_b):
    P = dict(w_mod=w_mod, b_mod=b_mod, w_in=w_in, rw_w0=rw_w0, rw_w1=rw_w1, rw_w2=rw_w2,
             rw_a0=rw_a0, rw_a1=rw_a1, rw_a2=rw_a2, rw_kk=rw_kk, rw_ka=rw_ka, rw_rk=rw_rk,
             rw_lnx_g=rw_lnx_g, rw_lnx_b=rw_lnx_b, mla_qnorm=mla_qnorm, mla_wuq=mla_wuq,
             mla_kvnorm=mla_kvnorm, mla_wukv=mla_wukv, lru_conv_w=lru_conv_w,
             lru_conv_b=lru_conv_b, lru_wa=lru_wa, lru_ba=lru_ba, lru_wx=lru_wx, lru_bx=lru_bx,
             lru_lambda=lru_lambda, w_out=w_out, ln_g=ln_g, ln_b=ln_b)
    xp = x_prompt
    rw_states, ckvs, krs, lru_states = [], [], [], []
    for l in range(DEPTH):
        xp, (S_l, ckv_l, kr_l, h_l) = trunk_layer(xp, c_ctx[None, :], l, P)
        rw_states.append(S_l)
        ckvs.append(ckv_l)
        krs.append(kr_l)
        lru_states.append(h_l)
    xs = x_sample
    for l in range(DEPTH):
        xs, _ = trunk_layer(xs, c, l, P, ctx=(state_rwkv[:, l], cache_mla_ckv[:, l],
                                              cache_mla_krope[:, l], state_lru[:, l]))
    new_state_rwkv = jnp.stack(rw_states, axis=1).astype(x_prompt.dtype)
    new_cache_mla_ckv = jnp.stack(ckvs, axis=1).astype(x_prompt.dtype)
    new_cache_mla_krope = jnp.stack(krs, axis=1).astype(x_prompt.dtype)
    new_state_lru = jnp.stack(lru_states, axis=1).astype(x_prompt.dtype)
    return (xp, xs, new_state_rwkv, new_cache_mla_ckv, new_cache_mla_krope, new_state_lru)
```

```python
import functools
import math

import jax
import jax.numpy as jnp
import numpy as np
from jax import lax
from jax.experimental import pallas as pl
from jax.experimental.pallas import tpu as pltpu

LANES = 128
SUBLANES = 8
VMEM_LIMIT_BYTES = 56 * 1024 * 1024

N_HEAD_RW = 4
HEAD_RW = 64
D_RW = N_HEAD_RW * HEAD_RW
MLA_HEADS = 8
MLA_NOPE = 64
MLA_ROPE = 32
MLA_V = 64
D_MLA = MLA_HEADS * MLA_V
Q_RANK = 256
KV_RANK = 128
D_LRU = 256
LRU_BLOCKS = 4
LRU_BLOCK = 64
GRID_W = 64
ROPE_BASE = 10000.0
SOFTMAX_SCALE = (MLA_NOPE + MLA_ROPE) ** -0.5
LRU_C = 8.0
GN_EPS = 64e-5
LN_EPS = 1e-5
RMS_EPS = 1e-6
SLOT = LANES
D_SLOTS = MLA_HEADS * SLOT

C_RKVG = 0
C_CQ = C_RKVG + 4 * D_RW
C_CKV = C_CQ + Q_RANK
C_KRA = C_CKV + KV_RANK
C_KRB = C_KRA + SLOT
C_GM = C_KRB + SLOT
C_XL = C_GM + D_MLA
C_GL = C_XL + D_LRU
C_LORA = C_GL + D_LRU
N_CAT = C_LORA + 4 * 64

R_W0, R_A0, R_KK, R_KA, R_RK = 0, 2, 4, 5, 6

RWK_R, RWK_NKK, RWK_V, RWK_DIR = 0, 1, 2, 3
N_RWK = 9 * D_RW


def _params(sem):
    return pltpu.CompilerParams(dimension_semantics=sem, vmem_limit_bytes=VMEM_LIMIT_BYTES)


def _const_spec(shape):
    zeros = (0,) * len(shape)
    return pl.BlockSpec(shape, lambda *_: zeros)


def _split3(x):
    hi = x.astype(jnp.bfloat16)
    r1 = x - hi.astype(jnp.float32)
    mid = r1.astype(jnp.bfloat16)
    lo = (r1 - mid.astype(jnp.float32)).astype(jnp.bfloat16)
    return hi, mid, lo


def _seg_sum(x, ones_bf16):
    hi, mid, lo = _split3(x)
    dot = functools.partial(jnp.dot, preferred_element_type=jnp.float32)
    return dot(hi, ones_bf16) + dot(mid, ones_bf16) + dot(lo, ones_bf16)


def _sigmoid(x):
    return 1.0 / (1.0 + jnp.exp(-x))


def _silu(x):
    return x * _sigmoid(x)


def _softplus(x):
    return jnp.maximum(x, 0.0) + jnp.log1p(jnp.exp(-jnp.abs(x)))


def _bdot(a, b):
    return jnp.dot(a.astype(jnp.bfloat16), b, preferred_element_type=jnp.float32)


def _mod_kernel(c_ref, w_ref, b_ref, o_ref):
    a = _silu(c_ref[...])
    w = w_ref[0]
    a_hi, a_mid, _ = _split3(a)
    w_hi, w_mid, _ = _split3(w)
    dot = functools.partial(jnp.dot, preferred_element_type=jnp.float32)
    o_ref[0] = dot(a_hi, w_hi) + dot(a_hi, w_mid) + dot(a_mid, w_hi) + b_ref[0]


def _modulation(cvecs, w_mod, b_mod):
    depth, d, d3 = w_mod.shape
    rows = cvecs.shape[0]
    nblk = d3 // d
    return pl.pallas_call(
        _mod_kernel,
        grid=(depth, nblk),
        in_specs=[
            pl.BlockSpec((rows, d), lambda l, j: (0, 0)),
            pl.BlockSpec((1, d, d), lambda l, j: (l, 0, j)),
            pl.BlockSpec((1, 1, d), lambda l, j: (l, 0, j)),
        ],
        out_specs=pl.BlockSpec((1, rows, d), lambda l, j: (l, 0, j)),
        out_shape=jax.ShapeDtypeStruct((depth, rows, d3), jnp.float32),
        compiler_params=_params(("parallel", "parallel")),
        name="adaln_mod",
    )(cvecs, w_mod, b_mod.reshape(depth, 1, d3))


def _pre_kernel(rope, *refs):
    if rope:
        (x_ref, mod_ref, win_ref, lw2_ref, la2_ref, rwp_ref, wq_ref, wkv_ref, qn_ref, kvn_ref,
         ones_ref, ta_ref, tb_ref, ca_ref, sb_ref,
         rwk_ref, rwe_ref, q_ref, k_ref, v_ref, gm_ref, xlgl_ref) = refs
    else:
        (x_ref, mod_ref, win_ref, lw2_ref, la2_ref, rwp_ref, wq_ref, wkv_ref, qn_ref, kvn_ref,
         ones_ref,
         rwk_ref, rwe_ref, q_ref, k_ref, v_ref, gm_ref, xlgl_ref, ckvn_ref, kr_ref) = refs

    x = x_ref[0]
    shift = mod_ref[0, 0:1, :]
    scale = mod_ref[0, 1:2, :]
    mu = jnp.mean(x, axis=-1, keepdims=True)
    xc = x - mu
    var = jnp.mean(xc * xc, axis=-1, keepdims=True)
    u = xc * lax.rsqrt(var + LN_EPS) * (1.0 + scale) + shift
    h = _bdot(u, win_ref[...])

    r = h[:, 0:D_RW]
    k = h[:, D_RW:2 * D_RW]
    v = h[:, 2 * D_RW:3 * D_RW]
    g = h[:, 3 * D_RW:4 * D_RW]
    lora_w = h[:, C_LORA:C_LORA + 128]
    lora_a = h[:, C_LORA + 128:C_LORA + 256]
    wl = _bdot(jnp.tanh(lora_w), lw2_ref[...])
    al = _bdot(lora_a, la2_ref[...])
    ones = ones_ref[...]
    kk = k * rwp_ref[R_KK:R_KK + 1, :]
    kk = kk / jnp.maximum(jnp.sqrt(_seg_sum(kk * kk, ones)), 1e-12)
    bonus = _seg_sum(r * k * rwp_ref[R_RK:R_RK + 1, :], ones) * v
    ka = rwp_ref[R_KA:R_KA + 1, :]
    rwk_ref[0, :, RWK_R * D_RW:(RWK_R + 1) * D_RW] = r
    rwk_ref[0, :, RWK_NKK * D_RW:(RWK_NKK + 1) * D_RW] = -kk
    rwk_ref[0, :, RWK_V * D_RW:(RWK_V + 1) * D_RW] = v
    for d in range(2):
        wl_d = wl[:, d * D_RW:(d + 1) * D_RW] + rwp_ref[R_W0 + d:R_W0 + d + 1, :]
        decay = jnp.exp(-math.exp(-0.5) * _sigmoid(wl_d))
        a_d = _sigmoid(al[:, d * D_RW:(d + 1) * D_RW] + rwp_ref[R_A0 + d:R_A0 + d + 1, :])
        base = (RWK_DIR + 3 * d) * D_RW
        rwk_ref[0, :, base:base + D_RW] = decay
        rwk_ref[0, :, base + D_RW:base + 2 * D_RW] = kk * a_d
        rwk_ref[0, :, base + 2 * D_RW:base + 3 * D_RW] = k * (1.0 + (a_d - 1.0) * ka)
    rwe_ref[0, :, 0:D_RW] = g
    rwe_ref[0, :, D_RW:2 * D_RW] = bonus

    cq = h[:, C_CQ:C_CQ + Q_RANK]
    cqn = cq * lax.rsqrt(jnp.mean(cq * cq, axis=-1, keepdims=True) + RMS_EPS) * qn_ref[...]
    ckv = h[:, C_CKV:C_CKV + KV_RANK]
    ckvn = ckv * lax.rsqrt(jnp.mean(ckv * ckv, axis=-1, keepdims=True) + RMS_EPS) * kvn_ref[...]
    qq = _bdot(cqn, wq_ref[...])
    kv = _bdot(ckvn, wkv_ref[...])
    kra = h[:, C_KRA:C_KRA + SLOT]
    krb = h[:, C_KRB:C_KRB + SLOT]
    if rope:
        k_rope = kra * ca_ref[...] + krb * sb_ref[...]
        ta = ta_ref[...]
        tb = tb_ref[...]
    else:
        k_rope = krb
        ckvn_ref[0] = ckvn
        kr_ref[0] = kra[:, 0:MLA_ROPE]
    for hd in range(MLA_HEADS):
        lo, hi = hd * SLOT, (hd + 1) * SLOT
        if rope:
            qh = qq[:, lo:hi] * ta + qq[:, D_SLOTS + lo:D_SLOTS + hi] * tb
        else:
            qh = qq[:, lo:hi] * SOFTMAX_SCALE
        q_ref[0, :, lo:hi] = qh.astype(jnp.bfloat16)
        k_ref[0, :, lo:hi] = (kv[:, lo:hi] + k_rope).astype(jnp.bfloat16)
    v_ref[0] = kv[:, D_SLOTS:2 * D_SLOTS].astype(jnp.bfloat16)
    gm_ref[0] = h[:, C_GM:C_GM + D_MLA]
    xlgl_ref[0] = h[:, C_XL:C_XL + 2 * D_LRU]


def _pre(x, mod, wts, rope_tabs, tile):
    b, l, d = x.shape
    rope = rope_tabs is not None
    nt = l // tile
    mod_b = mod.shape[0]
    tok = lambda width: pl.BlockSpec((1, tile, width), lambda bi, ti: (bi, ti, 0))
    in_specs = [
        tok(d),
        pl.BlockSpec((1, 3, d), (lambda bi, ti: (bi, 0, 0)) if mod_b > 1 else (lambda bi, ti: (0, 0, 0))),
        _const_spec(wts["w_cat"].shape), _const_spec(wts["lw2"].shape), _const_spec(wts["la2"].shape),
        _const_spec(wts["rwp"].shape), _const_spec(wts["wq"].shape), _const_spec(wts["wkv"].shape),
        _const_spec(wts["qnorm"].shape), _const_spec(wts["kvnorm"].shape), _const_spec(wts["ones"].shape),
    ]
    args = [x, mod, wts["w_cat"], wts["lw2"], wts["la2"], wts["rwp"], wts["wq"], wts["wkv"],
            wts["qnorm"], wts["kvnorm"], wts["ones"]]
    f32, bf16 = jnp.float32, jnp.bfloat16
    out_shape = [
        jax.ShapeDtypeStruct((b, l, N_RWK), f32), jax.ShapeDtypeStruct((b, l, 2 * D_RW), f32),
        jax.ShapeDtypeStruct((b, l, D_SLOTS), bf16), jax.ShapeDtypeStruct((b, l, D_SLOTS), bf16),
        jax.ShapeDtypeStruct((b, l, D_SLOTS), bf16), jax.ShapeDtypeStruct((b, l, D_MLA), f32),
        jax.ShapeDtypeStruct((b, l, 2 * D_LRU), f32),
    ]
    out_specs = [tok(N_RWK), tok(2 * D_RW), tok(D_SLOTS), tok(D_SLOTS), tok(D_SLOTS), tok(D_MLA), tok(2 * D_LRU)]
    if rope:
        tab = pl.BlockSpec((tile, SLOT), lambda bi, ti: (ti, 0))
        in_specs += [tab, tab, tab, tab]
        args += list(rope_tabs)
    else:
        out_shape += [jax.ShapeDtypeStruct((b, l, KV_RANK), f32), jax.ShapeDtypeStruct((b, l, MLA_ROPE), f32)]
        out_specs += [tok(KV_RANK), tok(MLA_ROPE)]
    return pl.pallas_call(
        functools.partial(_pre_kernel, rope),
        grid=(b, nt), in_specs=in_specs, out_specs=out_specs, out_shape=out_shape,
        compiler_params=_params(("parallel", "parallel")),
        name="pre_rope" if rope else "pre_ctx",
    )(*args)


KV_W, KV_B, KV_KD, KV_R, KV_NKK = 0, 1, 2, 3, 4


def _scan_kernel(steps, kv_ref, vv_ref, s0_ref, nkk0_ref, y_ref, sfin_ref, s_scr):
    i = pl.program_id(1)
    vr = vv_ref.shape[2]
    sa_row = HEAD_RW

    def row(ref, *idx):
        r = ref[idx[:-1] + (pl.ds(idx[-1], 1), slice(None))]
        return jnp.broadcast_to(r, (vr, LANES))

    @pl.when(i == 0)
    def _():
        s_scr[0:HEAD_RW] = s0_ref[0]
        sa = None
        for k in range(HEAD_RW):
            term = s0_ref[0, k] * row(nkk0_ref, 0, k)
            sa = term if sa is None else sa + term
        s_scr[sa_row] = sa

    def step(t, sa):
        vv = vv_ref[0, t]
        y = None
        sn = None
        for k in range(HEAD_RW):
            s = s_scr[k] * row(kv_ref, 0, t, KV_W * HEAD_RW + k) + (
                sa * row(kv_ref, 0, t, KV_B * HEAD_RW + k) + vv * row(kv_ref, 0, t, KV_KD * HEAD_RW + k))
            s_scr[k] = s
            ty = s * row(kv_ref, 0, t, KV_R * HEAD_RW + k)
            tn = s * row(kv_ref, 0, t, KV_NKK * HEAD_RW + k)
            y = ty if y is None else y + ty
            sn = tn if sn is None else sn + tn
        y_ref[0, t] = y
        return sn

    s_scr[sa_row] = lax.fori_loop(0, steps, step, s_scr[sa_row])

    @pl.when(i == pl.num_programs(1) - 1)
    def _():
        sfin_ref[0] = s_scr[0:HEAD_RW]


def _rwkv_scan(kv5, vv, s0, nkk0, steps):
    g, l, _, _ = kv5.shape
    vr = vv.shape[2]
    nt = l // steps
    return pl.pallas_call(
        functools.partial(_scan_kernel, steps),
        grid=(g, nt),
        in_specs=[
            pl.BlockSpec((1, steps, 5 * HEAD_RW, LANES), lambda gi, ti: (gi, ti, 0, 0)),
            pl.BlockSpec((1, steps, vr, LANES), lambda gi, ti: (gi, ti, 0, 0)),
            pl.BlockSpec((1, HEAD_RW, vr, LANES), lambda gi, ti: (gi, 0, 0, 0)),
            pl.BlockSpec((1, HEAD_RW, LANES), lambda gi, ti: (gi, 0, 0)),
        ],
        out_specs=[
            pl.BlockSpec((1, steps, vr, LANES), lambda gi, ti: (gi, ti, 0, 0)),
            pl.BlockSpec((1, HEAD_RW, vr, LANES), lambda gi, ti: (gi, 0, 0, 0)),
        ],
        out_shape=[
            jax.ShapeDtypeStruct((g, l, vr, LANES), jnp.float32),
            jax.ShapeDtypeStruct((g, HEAD_RW, vr, LANES), jnp.float32),
        ],
        scratch_shapes=[pltpu.VMEM((HEAD_RW + 1, vr, LANES), jnp.float32)],
        compiler_params=_params(("parallel", "arbitrary")),
        name="rwkv_scan",
    )(kv5, vv, s0, nkk0)


def _lane_plan(batch):
    per_dir = N_HEAD_RW * batch
    assert LANES % per_dir == 0 or per_dir == LANES
    fill = LANES // per_dir
    dl = min(2, fill)
    vs = fill // dl
    assert dl * vs * per_dir == LANES and HEAD_RW % (vs * SUBLANES) == 0
    return 2 // dl, dl, vs


def _to_kvec(x2, plan):
    g, dl, vs = plan
    _, b, l, _ = x2.shape
    t = x2.reshape(g, dl, b, l, N_HEAD_RW, HEAD_RW)
    t = jnp.transpose(t, (0, 3, 5, 1, 4, 2))
    t = jnp.broadcast_to(t[:, :, :, :, None], (g, l, HEAD_RW, dl, vs, N_HEAD_RW, b))
    return t.reshape(g, l, HEAD_RW, LANES)


def _to_vvec(x2, plan):
    g, dl, vs = plan
    _, b, l, _ = x2.shape
    vr = HEAD_RW // vs
    t = x2.reshape(g, dl, b, l, N_HEAD_RW, vs, vr)
    t = jnp.transpose(t, (0, 3, 6, 1, 5, 4, 2))
    return t.reshape(g, l, vr, LANES)


def _from_vvec(y, plan, b):
    g, dl, vs = plan
    _, l, vr, _ = y.shape
    t = y.reshape(g, l, vr, dl, vs, N_HEAD_RW, b)
    t = jnp.transpose(t, (0, 3, 6, 1, 5, 4, 2))
    return t.reshape(2, b, l, D_RW)


def _state_to_lanes(s, plan):
    g, dl, vs = plan
    b = s.shape[0]
    vr = HEAD_RW // vs
    t = s.reshape(b, g, dl, N_HEAD_RW, vs, vr, HEAD_RW)
    t = jnp.transpose(t, (1, 6, 5, 2, 4, 3, 0))
    return t.reshape(g, HEAD_RW, vr, LANES)


def _state_from_lanes(s, plan, b):
    g, dl, vs = plan
    vr = HEAD_RW // vs
    t = s.reshape(g, HEAD_RW, vr, dl, vs, N_HEAD_RW, b)
    t = jnp.transpose(t, (6, 0, 3, 5, 4, 2, 1))
    return t.reshape(b, 2, N_HEAD_RW, HEAD_RW, HEAD_RW)


def _rwkv(rwk, s0, steps):
    b, l, _ = rwk.shape
    plan = _lane_plan(b)
    col = lambda c: rwk[:, :, c * D_RW:(c + 1) * D_RW]
    both = lambda x: jnp.stack([x, jnp.flip(x, 1)])
    dirs = lambda c: jnp.stack([col(RWK_DIR + c), jnp.flip(col(RWK_DIR + 3 + c), 1)])
    nkk = both(col(RWK_NKK))
    nkk_next = jnp.concatenate([nkk[:, :, 1:], jnp.zeros_like(nkk[:, :, :1])], axis=2)
    kv5 = jnp.concatenate([_to_kvec(dirs(0), plan), _to_kvec(dirs(1), plan), _to_kvec(dirs(2), plan),
                           _to_kvec(both(col(RWK_R)), plan), _to_kvec(nkk_next, plan)], axis=2)
    nkk0 = _to_kvec(nkk[:, :, 0:1], plan)[:, 0]
    vv = _to_vvec(both(col(RWK_V)), plan)
    y, s_fin = _rwkv_scan(kv5, vv, _state_to_lanes(s0, plan), nkk0, steps)
    y = _from_vvec(y, plan, b)
    y = jnp.stack([y[0], jnp.flip(y[1], 1)])
    return y, _state_from_lanes(s_fin, plan, b)


def _attn_kernel(has_ctx, *refs):
    if has_ctx:
        q_ref, k_ref, v_ref, kc_ref, vc_ref, gm_ref, o_ref = refs
    else:
        q_ref, k_ref, v_ref, gm_ref, o_ref = refs
    nt = (((1,), (1,)), ((), ()))
    for pair in range(MLA_HEADS // 2):
        acc = None
        for hh in range(2):
            hd = 2 * pair + hh
            sl = slice(hd * SLOT, (hd + 1) * SLOT)
            q = q_ref[0, :, sl]
            s = lax.dot_general(q, k_ref[0, :, sl], nt, preferred_element_type=jnp.float32)
            m = jnp.max(s, axis=-1, keepdims=True)
            if has_ctx:
                sc = lax.dot_general(q, kc_ref[0, :, sl], nt, preferred_element_type=jnp.float32)
                m = jnp.maximum(m, jnp.max(sc, axis=-1, keepdims=True))
            p = jnp.exp(s - m)
            den = jnp.sum(p, axis=-1, keepdims=True)
            o = jnp.dot(p.astype(jnp.bfloat16), v_ref[0, :, sl], preferred_element_type=jnp.float32)
            if has_ctx:
                pc = jnp.exp(sc - m)
                den = den + jnp.sum(pc, axis=-1, keepdims=True)
                o = o + jnp.dot(pc.astype(jnp.bfloat16), vc_ref[0, :, sl], preferred_element_type=jnp.float32)
            o = o * (1.0 / den)
            acc = o if acc is None else acc + o
        cols = slice(pair * LANES, (pair + 1) * LANES)
        o_ref[0, :, cols] = acc * _silu(gm_ref[0, :, cols])


def _attention(q, k, v, gm, ctx_kv, tile):
    b, l, _ = q.shape
    nt = l // tile
    has_ctx = ctx_kv is not None
    tok = lambda width: pl.BlockSpec((1, tile, width), lambda bi, ti: (bi, ti, 0))
    full = lambda arr: pl.BlockSpec((1,) + arr.shape[1:], lambda bi, ti: (bi, 0, 0))
    in_specs = [tok(D_SLOTS), full(k), full(v)]
    args = [q, k, v]
    if has_ctx:
        in_specs += [full(ctx_kv[0]), full(ctx_kv[1])]
        args += list(ctx_kv)
    in_specs.append(tok(D_MLA))
    args.append(gm)
    return pl.pallas_call(
        functools.partial(_attn_kernel, has_ctx),
        grid=(b, nt), in_specs=in_specs, out_specs=tok(D_MLA),
        out_shape=jax.ShapeDtypeStruct((b, l, D_MLA), jnp.float32),
        compiler_params=_params(("parallel", "parallel")),
        name="mla_attn_ctx" if has_ctx else "mla_attn",
    )(*args)


def _ctxkv_kernel(ckv_ref, kr_ref, wkv_ref, k_ref, v_ref):
    kv = _bdot(ckv_ref[0], wkv_ref[...])
    kr = kr_ref[0]
    for hd in range(MLA_HEADS):
        sl = slice(hd * SLOT, (hd + 1) * SLOT)
        k_ref[0, :, sl] = (kv[:, sl] + kr).astype(jnp.bfloat16)
    v_ref[0] = kv[:, D_SLOTS:2 * D_SLOTS].astype(jnp.bfloat16)


def _ctx_keyvals(ckv_ctx, kr_slot, wkv):
    b, p, _ = ckv_ctx.shape
    blk = lambda width: pl.BlockSpec((1, p, width), lambda bi: (bi, 0, 0))
    return pl.pallas_call(
        _ctxkv_kernel, grid=(b,),
        in_specs=[blk(KV_RANK), blk(SLOT), _const_spec(wkv.shape)],
        out_specs=[blk(D_SLOTS), blk(D_SLOTS)],
        out_shape=[jax.ShapeDtypeStruct((b, p, D_SLOTS), jnp.bfloat16)] * 2,
        compiler_params=_params(("parallel",)),
        name="ctx_keyvals",
    )(ckv_ctx, kr_slot, wkv)


def _lru_kernel(xlgl_ref, cw_ref, cb_ref, wg_ref, bg_ref, lam_ref, h0_ref, y_ref, hfin_ref,
                a_scr, b_scr, h_scr):
    l = xlgl_ref.shape[1]
    xl = xlgl_ref[0, :, 0:D_LRU]
    t_idx = lax.broadcasted_iota(jnp.int32, (l, D_LRU), 0)
    xc = cb_ref[...] + cw_ref[2:3, :] * xl
    for j, back in ((0, 2), (1, 1), (3, -1)):
        rolled = pltpu.roll(xl, back % l, 0)
        ok = (t_idx >= back) if back > 0 else (t_idx < l + back)
        xc = xc + cw_ref[j:j + 1, :] * jnp.where(ok, rolled, 0.0)
    xcb = xc.astype(jnp.bfloat16)
    r_idx = lax.broadcasted_iota(jnp.int32, (SUBLANES, D_LRU), 0)
    nblk = l // SUBLANES
    for d in range(2):
        gr = jnp.dot(xcb, wg_ref[:, d * D_LRU:(d + 1) * D_LRU], preferred_element_type=jnp.float32)
        gr = gr + bg_ref[:, d * D_LRU:(d + 1) * D_LRU]
        gi = jnp.dot(xcb, wg_ref[:, (2 + d) * D_LRU:(3 + d) * D_LRU], preferred_element_type=jnp.float32)
        gi = gi + bg_ref[:, (2 + d) * D_LRU:(3 + d) * D_LRU]
        log_a = -LRU_C * _sigmoid(gr) * _softplus(-lam_ref[d:d + 1, :])
        a = jnp.exp(log_a)
        a_scr[...] = a
        b_scr[...] = jnp.sqrt(1.0 - a * a) * _sigmoid(gi) * xc

        def block(i, carry, d=d):
            blk = i if d == 0 else nblk - 1 - i
            rs = pl.ds(pl.multiple_of(blk * SUBLANES, SUBLANES), SUBLANES)
            aa = a_scr[rs, :]
            bb = b_scr[rs, :]
            for s in (1, 2, 4):
                if d == 0:
                    keep = r_idx >= s
                    sh = s
                else:
                    keep = r_idx < SUBLANES - s
                    sh = SUBLANES - s
                a_sh = jnp.where(keep, pltpu.roll(aa, sh, 0), 1.0)
                b_sh = jnp.where(keep, pltpu.roll(bb, sh, 0), 0.0)
                bb = bb + aa * b_sh
                aa = aa * a_sh
            hh = bb + aa * carry
            h_scr[d, rs, :] = hh
            edge = SUBLANES - 1 if d == 0 else 0
            return jnp.broadcast_to(hh[edge:edge + 1, :], (SUBLANES, D_LRU))

        last = lax.fori_loop(0, nblk, block, jnp.broadcast_to(h0_ref[0, d:d + 1, :], (SUBLANES, D_LRU)))
        hfin_ref[0, d:d + 1, :] = last[0:1, :]
    gl = xlgl_ref[0, :, D_LRU:2 * D_LRU]
    y_ref[0] = (h_scr[0] + h_scr[1]) * _silu(gl)


def _lru(xlgl, h0, wts):
    b, l, _ = xlgl.shape
    return pl.pallas_call(
        _lru_kernel, grid=(b,),
        in_specs=[
            pl.BlockSpec((1, l, 2 * D_LRU), lambda bi: (bi, 0, 0)),
            _const_spec(wts["conv_w"].shape), _const_spec(wts["conv_b"].shape),
            _const_spec(wts["wg"].shape), _const_spec(wts["bg"].shape), _const_spec(wts["lam"].shape),
            pl.BlockSpec((1, 2, D_LRU), lambda bi: (bi, 0, 0)),
        ],
        out_specs=[pl.BlockSpec((1, l, D_LRU), lambda bi: (bi, 0, 0)),
                   pl.BlockSpec((1, 2, D_LRU), lambda bi: (bi, 0, 0))],
        out_shape=[jax.ShapeDtypeStruct((b, l, D_LRU), jnp.float32),
                   jax.ShapeDtypeStruct((b, 2, D_LRU), jnp.float32)],
        scratch_shapes=[pltpu.VMEM((l, D_LRU), jnp.float32), pltpu.VMEM((l, D_LRU), jnp.float32),
                        pltpu.VMEM((2, l, D_LRU), jnp.float32)],
        compiler_params=_params(("parallel",)),
        name="rglru",
    )(xlgl, wts["conv_w"], wts["conv_b"], wts["wg"], wts["bg"], wts["lam"], h0)


def _post_kernel(alpha, x_ref, mod_ref, yrw_ref, rwe_ref, ymla_ref, ylru_ref, wo_ref, lnx_ref, ln_ref,
                 ones_ref, o_ref):
    y = yrw_ref[0, 0] + yrw_ref[1, 0]
    ones = ones_ref[...]
    inv_n = 1.0 / HEAD_RW
    mu = _seg_sum(y, ones) * inv_n
    yc = y - mu
    var = _seg_sum(yc * yc, ones) * inv_n
    yn = yc * lax.rsqrt(var + GN_EPS) * lnx_ref[0:1, :] + lnx_ref[1:2, :]
    y_rw = (yn + rwe_ref[0, :, D_RW:2 * D_RW]) * _silu(rwe_ref[0, :, 0:D_RW])
    out = (_bdot(y_rw, wo_ref[0:D_RW, :]) + _bdot(ymla_ref[0], wo_ref[D_RW:D_RW + D_MLA, :])
           + _bdot(ylru_ref[0], wo_ref[D_RW + D_MLA:D_RW + D_MLA + D_LRU, :]))
    z = alpha * x_ref[0] + mod_ref[0, 2:3, :] * out
    mu_z = jnp.mean(z, axis=-1, keepdims=True)
    zc = z - mu_z
    var_z = jnp.mean(zc * zc, axis=-1, keepdims=True)
    o_ref[0] = zc * lax.rsqrt(var_z + LN_EPS) * ln_ref[0:1, :] + ln_ref[1:2, :]


def _post(x, mod, yrw, rwe, ymla, ylru, wts, alpha, tile):
    b, l, d = x.shape
    nt = l // tile
    mod_b = mod.shape[0]
    tok = lambda width: pl.BlockSpec((1, tile, width), lambda bi, ti: (bi, ti, 0))
    return pl.pallas_call(
        functools.partial(_post_kernel, alpha),
        grid=(b, nt),
        in_specs=[
            tok(d),
            pl.BlockSpec((1, 3, d), (lambda bi, ti: (bi, 0, 0)) if mod_b > 1 else (lambda bi, ti: (0, 0, 0))),
            pl.BlockSpec((2, 1, tile, D_RW), lambda bi, ti: (0, bi, ti, 0)),
            tok(2 * D_RW), tok(D_MLA), tok(D_LRU),
            _const_spec(wts["w_out"].shape), _const_spec(wts["lnx"].shape), _const_spec(wts["ln"].shape),
            _const_spec(wts["ones"].shape),
        ],
        out_specs=tok(d),
        out_shape=jax.ShapeDtypeStruct((b, l, d), jnp.float32),
        compiler_params=_params(("parallel", "parallel")),
        name="post",
    )(x, mod, yrw, rwe, ymla, ylru, wts["w_out"], wts["lnx"], wts["ln"], wts["ones"])


def _block_diag(w):
    g, i, o = w.shape
    eye = jnp.eye(g, dtype=w.dtype)
    return jnp.einsum("gio,gh->giho", w, eye).reshape(g * i, g * o)


def _place(cols, offset, width):
    d, n = cols.shape
    return jnp.pad(cols, ((0, 0), (offset, width - offset - n)))


def _pack_layer(l, P, rope):
    bf16 = jnp.bfloat16
    w_in = P["w_in"][l]
    d = w_in.shape[0]
    o_cq = 4 * D_RW
    o_ckv = o_cq + Q_RANK
    o_kr = o_ckv + KV_RANK
    o_gm = o_kr + MLA_ROPE
    o_xl = o_gm + D_MLA
    w_kr = w_in[:, o_kr:o_kr + MLA_ROPE]
    half = MLA_ROPE // 2
    perm = np.concatenate([np.arange(half) * 2, np.arange(half) * 2 + 1])
    perm_sw = np.concatenate([np.arange(half) * 2 + 1, np.arange(half) * 2])
    if rope:
        kra = _place(w_kr[:, perm], MLA_NOPE, SLOT)
        krb = _place(w_kr[:, perm_sw], MLA_NOPE, SLOT)
    else:
        kra = _place(w_kr, 0, SLOT)
        krb = _place(w_kr, MLA_NOPE, SLOT)
    lora = jnp.concatenate([P["rw_w1"][l, 0], P["rw_w1"][l, 1], P["rw_a1"][l, 0], P["rw_a1"][l, 1]], axis=1)
    w_cat = jnp.concatenate([w_in[:, 0:o_kr], kra, krb, w_in[:, o_gm:], lora], axis=1)
    assert w_cat.shape == (d, N_CAT)

    wuq = P["mla_wuq"][l].reshape(Q_RANK, MLA_HEADS, MLA_NOPE + MLA_ROPE)
    qn, qr = wuq[..., :MLA_NOPE], wuq[..., MLA_NOPE:]
    pad_q = lambda rp: jnp.pad(jnp.concatenate([qn, rp], -1), ((0, 0), (0, 0), (0, SLOT - MLA_NOPE - MLA_ROPE)))
    if rope:
        wq_a = pad_q(qr[..., perm])
        wq_b = jnp.pad(qr[..., perm_sw], ((0, 0), (0, 0), (MLA_NOPE, SLOT - MLA_NOPE - MLA_ROPE)))
        wq = jnp.concatenate([wq_a.reshape(Q_RANK, D_SLOTS), wq_b.reshape(Q_RANK, D_SLOTS)], axis=1)
    else:
        wq = pad_q(qr).reshape(Q_RANK, D_SLOTS)
    wukv = P["mla_wukv"][l].reshape(KV_RANK, MLA_HEADS, MLA_NOPE + MLA_V)
    wk = jnp.pad(wukv[..., :MLA_NOPE], ((0, 0), (0, 0), (0, SLOT - MLA_NOPE)))
    wv = wukv[..., MLA_NOPE:]
    odd = (jnp.arange(MLA_HEADS) % 2 == 1)[None, :, None]
    wv = jnp.where(odd, jnp.pad(wv, ((0, 0), (0, 0), (MLA_V, 0))), jnp.pad(wv, ((0, 0), (0, 0), (0, MLA_V))))
    wkv = jnp.concatenate([wk.reshape(KV_RANK, D_SLOTS), wv.reshape(KV_RANK, D_SLOTS)], axis=1)

    zeros_w2 = jnp.zeros_like(P["rw_w2"][l, 0])
    lw2 = jnp.block([[P["rw_w2"][l, 0], zeros_w2], [zeros_w2, P["rw_w2"][l, 1]]])
    la2 = jnp.block([[P["rw_a2"][l, 0], zeros_w2], [zeros_w2, P["rw_a2"][l, 1]]])
    rwp = jnp.concatenate([P["rw_w0"][l], P["rw_a0"][l], P["rw_kk"][l][None], P["rw_ka"][l][None],
                           P["rw_rk"][l].reshape(1, D_RW), jnp.zeros((1, D_RW), jnp.float32)], axis=0)
    grp = jnp.arange(D_RW) // HEAD_RW
    ones = (grp[:, None] == grp[None, :]).astype(bf16)
    wg = jnp.concatenate([_block_diag(P["lru_wa"][l, 0]), _block_diag(P["lru_wa"][l, 1]),
                          _block_diag(P["lru_wx"][l, 0]), _block_diag(P["lru_wx"][l, 1])], axis=1)
    bg = jnp.concatenate([P["lru_ba"][l, 0], P["lru_ba"][l, 1], P["lru_bx"][l, 0], P["lru_bx"][l, 1]])[None]
    return {
        "w_cat": w_cat.astype(bf16), "lw2": lw2.astype(bf16), "la2": la2.astype(bf16), "rwp": rwp,
        "wq": wq.astype(bf16), "wkv": wkv.astype(bf16),
        "qnorm": P["mla_qnorm"][l][None], "kvnorm": P["mla_kvnorm"][l][None], "ones": ones,
        "conv_w": P["lru_conv_w"][l], "conv_b": P["lru_conv_b"][l][None],
        "wg": wg.astype(bf16), "bg": bg, "lam": P["lru_lambda"][l],
        "w_out": P["w_out"][l].astype(bf16),
        "lnx": jnp.stack([P["rw_lnx_g"][l], P["rw_lnx_b"][l]]),
        "ln": jnp.stack([P["ln_g"][l], P["ln_b"][l]]),
        "perm": perm,
    }


def _rope_tables(l):
    rows = l // GRID_W
    row = jnp.repeat(jnp.arange(rows), GRID_W).astype(jnp.float32)
    col = jnp.tile(jnp.arange(GRID_W), rows).astype(jnp.float32)
    pairs = MLA_ROPE // 4
    inv = ROPE_BASE ** (-jnp.arange(pairs, dtype=jnp.float32) / pairs)
    ang = jnp.concatenate([row[:, None] * inv, col[:, None] * inv], axis=-1)
    cos, sin = jnp.cos(ang), jnp.sin(ang)
    tail = jnp.zeros((l, SLOT - MLA_NOPE - MLA_ROPE), jnp.float32)
    nope0 = jnp.zeros((l, MLA_NOPE), jnp.float32)
    ca = jnp.concatenate([nope0, cos, cos, tail], axis=1)
    sb = jnp.concatenate([nope0, -sin, sin, tail], axis=1)
    ta = jnp.concatenate([jnp.ones((l, MLA_NOPE), jnp.float32), cos, cos, tail], axis=1) * SOFTMAX_SCALE
    tb = sb * SOFTMAX_SCALE
    return ta, tb, ca, sb


def _layer(x, mod, wts, alpha, ctx, rope_tabs):
    b, l, _ = x.shape
    tile = min(l, 256)
    outs = _pre(x, mod, wts, rope_tabs, tile)
    rwk, rwe, q, k, v, gm, xlgl = outs[:7]
    if ctx is None:
        s0 = jnp.zeros((b, 2, N_HEAD_RW, HEAD_RW, HEAD_RW), jnp.float32)
        h0 = jnp.zeros((b, 2, D_LRU), jnp.float32)
        ctx_kv = None
    else:
        s0, ckv_ctx, kr_ctx, h0 = ctx
        kr_slot = jnp.pad(kr_ctx[..., wts["perm"]], ((0, 0), (0, 0), (MLA_NOPE, SLOT - MLA_NOPE - MLA_ROPE)))
        ctx_kv = _ctx_keyvals(ckv_ctx, kr_slot, wts["wkv"])
    yrw, s_fin = _rwkv(rwk, s0, steps=16)
    ymla = _attention(q, k, v, gm, ctx_kv, tile)
    ylru, h_fin = _lru(xlgl, h0, wts)
    x_new = _post(x, mod, yrw, rwe, ymla, ylru, wts, alpha, tile)
    extras = (s_fin, outs[7], outs[8], h_fin) if ctx is None else None
    return x_new, extras


def kernel(x_prompt, x_sample, state_rwkv, cache_mla_ckv, cache_mla_krope, state_lru, c, c_ctx, w_mod, b_mod,
           w_in, rw_w0, rw_w1, rw_w2, rw_a0, rw_a1, rw_a2, rw_kk, rw_ka, rw_rk, rw_lnx_g, rw_lnx_b, mla_qnorm,
           mla_wuq, mla_kvnorm, mla_wukv, lru_conv_w, lru_conv_b, lru_wa, lru_ba, lru_wx, lru_bx, lru_lambda,
           w_out, ln_g, ln_b):
    P = dict(w_in=w_in, rw_w0=rw_w0, rw_w1=rw_w1, rw_w2=rw_w2, rw_a0=rw_a0, rw_a1=rw_a1, rw_a2=rw_a2,
             rw_kk=rw_kk, rw_ka=rw_ka, rw_rk=rw_rk, rw_lnx_g=rw_lnx_g, rw_lnx_b=rw_lnx_b, mla_qnorm=mla_qnorm,
             mla_wuq=mla_wuq, mla_kvnorm=mla_kvnorm, mla_wukv=mla_wukv, lru_conv_w=lru_conv_w,
             lru_conv_b=lru_conv_b, lru_wa=lru_wa, lru_ba=lru_ba, lru_wx=lru_wx, lru_bx=lru_bx,
             lru_lambda=lru_lambda, w_out=w_out, ln_g=ln_g, ln_b=ln_b)
    depth = w_in.shape[0]
    alpha = (2 * depth) ** 0.25
    dec_b, dec_l, d = x_sample.shape

    rows = -(-(1 + dec_b) // SUBLANES) * SUBLANES
    cvecs = jnp.concatenate([c_ctx[None, :], c, jnp.zeros((rows - 1 - dec_b, d), jnp.float32)], axis=0)
    mod_all = _modulation(cvecs, w_mod, b_mod).reshape(depth, rows, 3, d)
    rope_tabs = _rope_tables(dec_l)

    xp, xs = x_prompt, x_sample
    rw_states, ckvs, krs, lru_states = [], [], [], []
    for l in range(depth):
        xp, (s_l, ckv_l, kr_l, h_l) = _layer(xp, mod_all[l, 0:1], _pack_layer(l, P, rope=False), alpha, None, None)
        rw_states.append(s_l)
        ckvs.append(ckv_l)
        krs.append(kr_l)
        lru_states.append(h_l)
    for l in range(depth):
        ctx = (state_rwkv[:, l], cache_mla_ckv[:, l], cache_mla_krope[:, l], state_lru[:, l])
        xs, _ = _layer(xs, mod_all[l, 1:1 + dec_b], _pack_layer(l, P, rope=True), alpha, ctx, rope_tabs)
    return (xp, xs, jnp.stack(rw_states, axis=1), jnp.stack(ckvs, axis=1), jnp.stack(krs, axis=1),
            jnp.stack(lru_states, axis=1))
```

```python
import functools
import math

import jax
import jax.numpy as jnp
import numpy as np
from jax import lax
from jax.experimental import pallas as pl
from jax.experimental.pallas import tpu as pltpu

LANES = 128
SUBLANES = 8
VMEM_LIMIT_BYTES = 56 * 1024 * 1024

N_HEAD_RW = 4
HEAD_RW = 64
D_RW = N_HEAD_RW * HEAD_RW
MLA_HEADS = 8
MLA_NOPE = 64
MLA_ROPE = 32
MLA_V = 64
D_MLA = MLA_HEADS * MLA_V
Q_RANK = 256
KV_RANK = 128
D_LRU = 256
LRU_BLOCKS = 4
LRU_BLOCK = 64
GRID_W = 64
ROPE_BASE = 10000.0
SOFTMAX_SCALE = (MLA_NOPE + MLA_ROPE) ** -0.5
LRU_C = 8.0
GN_EPS = 64e-5
LN_EPS = 1e-5
RMS_EPS = 1e-6
SLOT = LANES
D_SLOTS = MLA_HEADS * SLOT

C_RKVG = 0
C_CQ = C_RKVG + 4 * D_RW
C_CKV = C_CQ + Q_RANK
C_KRA = C_CKV + KV_RANK
C_KRB = C_KRA + SLOT
C_GM = C_KRB + SLOT
C_XL = C_GM + D_MLA
C_GL = C_XL + D_LRU
C_LORA = C_GL + D_LRU
N_CAT = C_LORA + 4 * 64

R_W0, R_A0, R_KK, R_KA, R_RK = 0, 2, 4, 5, 6

SCAN_STEPS = 16
RWK_R, RWK_NKK, RWK_V, RWK_DIR = 0, 1, 2, 3
N_RWK_CHUNKS = 2 * 9
RWK_SHARED_CHUNKS = 2 * 3


def _params(sem):
    return pltpu.CompilerParams(dimension_semantics=sem, vmem_limit_bytes=VMEM_LIMIT_BYTES)


def _const_spec(shape):
    zeros = (0,) * len(shape)
    return pl.BlockSpec(shape, lambda *_: zeros)


def _split3(x):
    hi = x.astype(jnp.bfloat16)
    r1 = x - hi.astype(jnp.float32)
    mid = r1.astype(jnp.bfloat16)
    lo = (r1 - mid.astype(jnp.float32)).astype(jnp.bfloat16)
    return hi, mid, lo


def _seg_sum(x, ones_bf16):
    hi, mid, lo = _split3(x)
    dot = functools.partial(jnp.dot, preferred_element_type=jnp.float32)
    return dot(hi, ones_bf16) + dot(mid, ones_bf16) + dot(lo, ones_bf16)


def _sigmoid(x):
    return 1.0 / (1.0 + jnp.exp(-x))


def _silu(x):
    return x * _sigmoid(x)


def _softplus(x):
    return jnp.maximum(x, 0.0) + jnp.log1p(jnp.exp(-jnp.abs(x)))


def _bdot(a, b):
    return jnp.dot(a.astype(jnp.bfloat16), b, preferred_element_type=jnp.float32)


def _mod_kernel(c_ref, w_ref, b_ref, o_ref):
    a = _silu(c_ref[...])
    w = w_ref[0]
    a_hi, a_mid, _ = _split3(a)
    w_hi, w_mid, _ = _split3(w)
    dot = functools.partial(jnp.dot, preferred_element_type=jnp.float32)
    o_ref[0] = dot(a_hi, w_hi) + dot(a_hi, w_mid) + dot(a_mid, w_hi) + b_ref[0]


def _modulation(cvecs, w_mod, b_mod):
    depth, d, d3 = w_mod.shape
    rows = cvecs.shape[0]
    nblk = d3 // d
    return pl.pallas_call(
        _mod_kernel,
        grid=(depth, nblk),
        in_specs=[
            pl.BlockSpec((rows, d), lambda l, j: (0, 0)),
            pl.BlockSpec((1, d, d), lambda l, j: (l, 0, j)),
            pl.BlockSpec((1, 1, d), lambda l, j: (l, 0, j)),
        ],
        out_specs=pl.BlockSpec((1, rows, d), lambda l, j: (l, 0, j)),
        out_shape=jax.ShapeDtypeStruct((depth, rows, d3), jnp.float32),
        compiler_params=_params(("parallel", "parallel")),
        name="adaln_mod",
    )(cvecs, w_mod, b_mod.reshape(depth, 1, d3))


def _pre_kernel(rope, *refs):
    if rope:
        (x_ref, mod_ref, win_ref, lw2_ref, la2_ref, rwp_ref, wq_ref, wkv_ref, qn_ref, kvn_ref,
         ones_ref, ta_ref, tb_ref, ca_ref, sb_ref,
         rwk_ref, rwe_ref, q_ref, k_ref, v_ref, gm_ref, xlgl_ref) = refs
    else:
        (x_ref, mod_ref, win_ref, lw2_ref, la2_ref, rwp_ref, wq_ref, wkv_ref, qn_ref, kvn_ref,
         ones_ref,
         rwk_ref, rwe_ref, q_ref, k_ref, v_ref, gm_ref, xlgl_ref, ckvn_ref, kr_ref) = refs

    x = x_ref[0]
    shift = mod_ref[0, 0:1, :]
    scale = mod_ref[0, 1:2, :]
    mu = jnp.mean(x, axis=-1, keepdims=True)
    xc = x - mu
    var = jnp.mean(xc * xc, axis=-1, keepdims=True)
    u = xc * lax.rsqrt(var + LN_EPS) * (1.0 + scale) + shift
    h = _bdot(u, win_ref[...])

    r = h[:, 0:D_RW]
    k = h[:, D_RW:2 * D_RW]
    v = h[:, 2 * D_RW:3 * D_RW]
    g = h[:, 3 * D_RW:4 * D_RW]
    lora_w = h[:, C_LORA:C_LORA + 128]
    lora_a = h[:, C_LORA + 128:C_LORA + 256]
    wl = _bdot(jnp.tanh(lora_w), lw2_ref[...])
    al = _bdot(lora_a, la2_ref[...])
    ones = ones_ref[...]
    kk = k * rwp_ref[R_KK:R_KK + 1, :]
    kk = kk / jnp.maximum(jnp.sqrt(_seg_sum(kk * kk, ones)), 1e-12)
    bonus = _seg_sum(r * k * rwp_ref[R_RK:R_RK + 1, :], ones) * v
    ka = rwp_ref[R_KA:R_KA + 1, :]
    def put(slot, val):
        for tb in range(val.shape[0] // SCAN_STEPS):
            for half in range(D_RW // LANES):
                rwk_ref[tb, 2 * slot + half, 0] = val[tb * SCAN_STEPS:(tb + 1) * SCAN_STEPS,
                                                      half * LANES:(half + 1) * LANES]

    put(RWK_R, r)
    put(RWK_NKK, -kk)
    put(RWK_V, v)
    for d in range(2):
        wl_d = wl[:, d * D_RW:(d + 1) * D_RW] + rwp_ref[R_W0 + d:R_W0 + d + 1, :]
        decay = jnp.exp(-math.exp(-0.5) * _sigmoid(wl_d))
        a_d = _sigmoid(al[:, d * D_RW:(d + 1) * D_RW] + rwp_ref[R_A0 + d:R_A0 + d + 1, :])
        put(RWK_DIR + 3 * d, decay)
        put(RWK_DIR + 3 * d + 1, kk * a_d)
        put(RWK_DIR + 3 * d + 2, k * (1.0 + (a_d - 1.0) * ka))
    rwe_ref[0, :, 0:D_RW] = g
    rwe_ref[0, :, D_RW:2 * D_RW] = bonus

    cq = h[:, C_CQ:C_CQ + Q_RANK]
    cqn = cq * lax.rsqrt(jnp.mean(cq * cq, axis=-1, keepdims=True) + RMS_EPS) * qn_ref[...]
    ckv = h[:, C_CKV:C_CKV + KV_RANK]
    ckvn = ckv * lax.rsqrt(jnp.mean(ckv * ckv, axis=-1, keepdims=True) + RMS_EPS) * kvn_ref[...]
    qq = _bdot(cqn, wq_ref[...])
    kv = _bdot(ckvn, wkv_ref[...])
    kra = h[:, C_KRA:C_KRA + SLOT]
    krb = h[:, C_KRB:C_KRB + SLOT]
    if rope:
        k_rope = kra * ca_ref[...] + krb * sb_ref[...]
        ta = ta_ref[...]
        tb = tb_ref[...]
    else:
        k_rope = krb
        ckvn_ref[0] = ckvn
        kr_ref[0] = kra[:, 0:MLA_ROPE]
    for hd in range(MLA_HEADS):
        lo, hi = hd * SLOT, (hd + 1) * SLOT
        if rope:
            qh = qq[:, lo:hi] * ta + qq[:, D_SLOTS + lo:D_SLOTS + hi] * tb
        else:
            qh = qq[:, lo:hi] * SOFTMAX_SCALE
        q_ref[0, :, lo:hi] = qh.astype(jnp.bfloat16)
        k_ref[0, :, lo:hi] = (kv[:, lo:hi] + k_rope).astype(jnp.bfloat16)
    v_ref[0] = kv[:, D_SLOTS:2 * D_SLOTS].astype(jnp.bfloat16)
    gm_ref[0] = h[:, C_GM:C_GM + D_MLA]
    xlgl_ref[0] = h[:, C_XL:C_XL + 2 * D_LRU]


def _pre(x, mod, wts, rope_tabs, tile):
    b, l, d = x.shape
    rope = rope_tabs is not None
    nt = l // tile
    mod_b = mod.shape[0]
    tok = lambda width: pl.BlockSpec((1, tile, width), lambda bi, ti: (bi, ti, 0))
    in_specs = [
        tok(d),
        pl.BlockSpec((1, 3, d), (lambda bi, ti: (bi, 0, 0)) if mod_b > 1 else (lambda bi, ti: (0, 0, 0))),
        _const_spec(wts["w_cat"].shape), _const_spec(wts["lw2"].shape), _const_spec(wts["la2"].shape),
        _const_spec(wts["rwp"].shape), _const_spec(wts["wq"].shape), _const_spec(wts["wkv"].shape),
        _const_spec(wts["qnorm"].shape), _const_spec(wts["kvnorm"].shape), _const_spec(wts["ones"].shape),
    ]
    args = [x, mod, wts["w_cat"], wts["lw2"], wts["la2"], wts["rwp"], wts["wq"], wts["wkv"],
            wts["qnorm"], wts["kvnorm"], wts["ones"]]
    f32, bf16 = jnp.float32, jnp.bfloat16
    out_shape = [
        jax.ShapeDtypeStruct((l // SCAN_STEPS, N_RWK_CHUNKS, b, SCAN_STEPS, LANES), f32),
        jax.ShapeDtypeStruct((b, l, 2 * D_RW), f32),
        jax.ShapeDtypeStruct((b, l, D_SLOTS), bf16), jax.ShapeDtypeStruct((b, l, D_SLOTS), bf16),
        jax.ShapeDtypeStruct((b, l, D_SLOTS), bf16), jax.ShapeDtypeStruct((b, l, D_MLA), f32),
        jax.ShapeDtypeStruct((b, l, 2 * D_LRU), f32),
    ]
    rwk_spec = pl.BlockSpec((tile // SCAN_STEPS, N_RWK_CHUNKS, 1, SCAN_STEPS, LANES),
                            lambda bi, ti: (ti, 0, bi, 0, 0))
    out_specs = [rwk_spec, tok(2 * D_RW), tok(D_SLOTS), tok(D_SLOTS), tok(D_SLOTS), tok(D_MLA), tok(2 * D_LRU)]
    if rope:
        tab = pl.BlockSpec((tile, SLOT), lambda bi, ti: (ti, 0))
        in_specs += [tab, tab, tab, tab]
        args += list(rope_tabs)
    else:
        out_shape += [jax.ShapeDtypeStruct((b, l, KV_RANK), f32), jax.ShapeDtypeStruct((b, l, MLA_ROPE), f32)]
        out_specs += [tok(KV_RANK), tok(MLA_ROPE)]
    return pl.pallas_call(
        functools.partial(_pre_kernel, rope),
        grid=(b, nt), in_specs=in_specs, out_specs=out_specs, out_shape=out_shape,
        compiler_params=_params(("parallel", "parallel")),
        name="pre_rope" if rope else "pre_ctx",
    )(*args)


KT_R, KT_NKK, KT_V, KT_W, KT_B, KT_KD = range(6)
N_KT = 6
PLANES = 2


def _scan_kernel(dl, vs, steps, *refs):
    sh_refs, dr_refs = refs[0:dl], refs[dl:2 * dl]
    s0_ref = refs[2 * dl]
    y_refs = refs[2 * dl + 1:3 * dl + 1]
    sfin_ref = refs[3 * dl + 1]
    s_scr = refs[3 * dl + 2]
    kt_scrs = refs[3 * dl + 3:]
    g = pl.program_id(0)
    i = pl.program_id(1)
    rows_per_chunk = sh_refs[0].shape[1] // RWK_SHARED_CHUNKS
    batch = rows_per_chunk // steps
    vr = HEAD_RW // vs
    lane_split = (lax.broadcasted_iota(jnp.int32, (vr, LANES), 1) // (2 * batch)) % vs

    def block_row(e, j):
        if dl == 2:
            return j if e == 0 else steps - 1 - j
        return j + g * (steps - 1 - 2 * j)

    def batch_rows(ref, chunk, t):
        return ref[0, pl.ds(chunk * rows_per_chunk + t, batch, stride=steps), :]

    def gather(in_refs, slot, j):
        parts = []
        for e in range(dl):
            t = block_row(e, j)
            pair = [batch_rows(in_refs[e], 2 * slot + hp, t) for hp in range(2)]
            parts += pair * vs
        return jnp.concatenate(parts, axis=0)

    def prepare(j, kt):
        kt[KT_R] = gather(sh_refs, RWK_R, j).T
        kt[KT_NKK] = gather(sh_refs, RWK_NKK, j).T
        kt[KT_V] = gather(sh_refs, RWK_V, j).T
        kt[KT_W] = gather(dr_refs, 0, j).T
        kt[KT_B] = gather(dr_refs, 1, j).T
        kt[KT_KD] = gather(dr_refs, 2, j).T

    @pl.when(i == 0)
    def _():
        s_scr[...] = s0_ref[0]

    prepare(0, kt_scrs[0])

    def step(j, kt, kt_next):
        prepare(jnp.minimum(j + 1, steps - 1), kt_next)

        def row(tile, r):
            return jnp.broadcast_to(kt[tile, pl.ds(r, 1), :], (vr, LANES))

        y_planes = []
        for p in range(PLANES):
            base = p * HEAD_RW
            vv = kt[KT_V, base:base + vr, :]
            for s in range(1, vs):
                vv = jnp.where(lane_split == s, kt[KT_V, base + s * vr:base + (s + 1) * vr, :], vv)
            sa = None
            for k in range(HEAD_RW):
                term = s_scr[p, k] * row(KT_NKK, base + k)
                sa = term if sa is None else sa + term
            y = None
            for k in range(HEAD_RW):
                s_new = s_scr[p, k] * row(KT_W, base + k) + (sa * row(KT_B, base + k) + vv * row(KT_KD, base + k))
                s_scr[p, k] = s_new
                term = s_new * row(KT_R, base + k)
                y = term if y is None else y + term
            y_planes.append(y)
        zero = jnp.zeros((vr, LANES), jnp.float32)
        y_full = jnp.concatenate([jnp.where(lane_split == s, y_planes[p], zero)
                                  for p in range(PLANES) for s in range(vs)], axis=0)
        y_t = y_full.T
        for e in range(dl):
            t = block_row(e, j)
            for hp in range(2):
                acc = None
                for s in range(vs):
                    r0 = ((e * vs + s) * 2 + hp) * batch
                    part = y_t[r0:r0 + batch, :]
                    acc = part if acc is None else acc + part
                y_refs[e][0, 0, pl.ds(hp * rows_per_chunk + t, batch, stride=steps), :] = acc

    def two_steps(jj, carry):
        step(2 * jj, kt_scrs[0], kt_scrs[1])
        step(2 * jj + 1, kt_scrs[1], kt_scrs[0])
        return carry

    lax.fori_loop(0, steps // 2, two_steps, 0)

    @pl.when(i == pl.num_programs(1) - 1)
    def _():
        sfin_ref[0] = s_scr[...]


def _lane_plan(batch):
    per_dir = (N_HEAD_RW // PLANES) * batch
    assert LANES % per_dir == 0
    fill = LANES // per_dir
    dl = min(2, fill)
    vs = fill // dl
    assert dl * vs * per_dir == LANES and HEAD_RW % (vs * SUBLANES) == 0
    return 2 // dl, dl, vs


def _rwkv_scan(rwk, s0_lanes, plan):
    groups, dl, vs = plan
    nt, chunks, b, steps, _ = rwk.shape
    vr = HEAD_RW // vs
    rows = b * steps
    rwk_rows = rwk.reshape(nt, chunks * rows, LANES)

    def time_block(d, ti):
        return ti + d * (nt - 1 - 2 * ti)

    def slot_dir(e, gi):
        return e if dl == 2 else gi

    blk = (1, RWK_SHARED_CHUNKS * rows, LANES)
    in_specs = [pl.BlockSpec(blk, functools.partial(lambda e, gi, ti: (time_block(slot_dir(e, gi), ti), 0, 0), e))
                for e in range(dl)]
    in_specs += [pl.BlockSpec(blk, functools.partial(
        lambda e, gi, ti: (time_block(slot_dir(e, gi), ti), 1 + slot_dir(e, gi), 0), e)) for e in range(dl)]
    state_spec = pl.BlockSpec((1, PLANES, HEAD_RW, vr, LANES), lambda gi, ti: (gi, 0, 0, 0, 0))
    in_specs.append(state_spec)
    y_specs = [pl.BlockSpec((1, 1, 2 * rows, LANES), functools.partial(
        lambda e, gi, ti: (gi, time_block(slot_dir(e, gi), ti), 0, 0), e)) for e in range(dl)]
    outs = pl.pallas_call(
        functools.partial(_scan_kernel, dl, vs, steps),
        grid=(groups, nt),
        in_specs=in_specs,
        out_specs=y_specs + [state_spec],
        out_shape=[jax.ShapeDtypeStruct((groups, nt, 2 * rows, LANES), jnp.float32)] * dl
        + [jax.ShapeDtypeStruct((groups, PLANES, HEAD_RW, vr, LANES), jnp.float32)],
        scratch_shapes=[pltpu.VMEM((PLANES, HEAD_RW, vr, LANES), jnp.float32),
                        pltpu.VMEM((N_KT, LANES, LANES), jnp.float32),
                        pltpu.VMEM((N_KT, LANES, LANES), jnp.float32)],
        compiler_params=_params(("parallel", "arbitrary")),
        name="rwkv_scan",
    )(*([rwk_rows] * (2 * dl)), s0_lanes)
    ys = [y.reshape(groups, nt, 2, b, steps, LANES) for y in outs[:dl]]
    return ys, outs[dl]


def _state_to_lanes(s, plan):
    g, dl, vs = plan
    b = s.shape[0]
    vr = HEAD_RW // vs
    t = s.reshape(b, g, dl, N_HEAD_RW // PLANES, PLANES, vs, vr, HEAD_RW)
    t = jnp.transpose(t, (1, 4, 7, 6, 2, 5, 3, 0))
    return t.reshape(g, PLANES, HEAD_RW, vr, LANES)


def _state_from_lanes(s, plan, b):
    g, dl, vs = plan
    vr = HEAD_RW // vs
    t = s.reshape(g, PLANES, HEAD_RW, vr, dl, vs, N_HEAD_RW // PLANES, b)
    t = jnp.transpose(t, (7, 0, 4, 6, 1, 5, 3, 2))
    return t.reshape(b, 2, N_HEAD_RW, HEAD_RW, HEAD_RW)


def _rwkv(rwk, s0):
    b = rwk.shape[2]
    plan = _lane_plan(b)
    ys, s_fin = _rwkv_scan(rwk, _state_to_lanes(s0, plan), plan)
    return ys, _state_from_lanes(s_fin, plan, b)


def _attn_kernel(has_ctx, *refs):
    if has_ctx:
        q_ref, k_ref, v_ref, kc_ref, vc_ref, gm_ref, o_ref = refs
    else:
        q_ref, k_ref, v_ref, gm_ref, o_ref = refs
    nt = (((1,), (1,)), ((), ()))
    for pair in range(MLA_HEADS // 2):
        acc = None
        for hh in range(2):
            hd = 2 * pair + hh
            sl = slice(hd * SLOT, (hd + 1) * SLOT)
            q = q_ref[0, :, sl]
            s = lax.dot_general(q, k_ref[0, :, sl], nt, preferred_element_type=jnp.float32)
            m = jnp.max(s, axis=-1, keepdims=True)
            if has_ctx:
                sc = lax.dot_general(q, kc_ref[0, :, sl], nt, preferred_element_type=jnp.float32)
                m = jnp.maximum(m, jnp.max(sc, axis=-1, keepdims=True))
            p = jnp.exp(s - m)
            den = jnp.sum(p, axis=-1, keepdims=True)
            o = jnp.dot(p.astype(jnp.bfloat16), v_ref[0, :, sl], preferred_element_type=jnp.float32)
            if has_ctx:
                pc = jnp.exp(sc - m)
                den = den + jnp.sum(pc, axis=-1, keepdims=True)
                o = o + jnp.dot(pc.astype(jnp.bfloat16), vc_ref[0, :, sl], preferred_element_type=jnp.float32)
            o = o * (1.0 / den)
            acc = o if acc is None else acc + o
        cols = slice(pair * LANES, (pair + 1) * LANES)
        o_ref[0, :, cols] = acc * _silu(gm_ref[0, :, cols])


def _attention(q, k, v, gm, ctx_kv, tile):
    b, l, _ = q.shape
    nt = l // tile
    has_ctx = ctx_kv is not None
    tok = lambda width: pl.BlockSpec((1, tile, width), lambda bi, ti: (bi, ti, 0))
    full = lambda arr: pl.BlockSpec((1,) + arr.shape[1:], lambda bi, ti: (bi, 0, 0))
    in_specs = [tok(D_SLOTS), full(k), full(v)]
    args = [q, k, v]
    if has_ctx:
        in_specs += [full(ctx_kv[0]), full(ctx_kv[1])]
        args += list(ctx_kv)
    in_specs.append(tok(D_MLA))
    args.append(gm)
    return pl.pallas_call(
        functools.partial(_attn_kernel, has_ctx),
        grid=(b, nt), in_specs=in_specs, out_specs=tok(D_MLA),
        out_shape=jax.ShapeDtypeStruct((b, l, D_MLA), jnp.float32),
        compiler_params=_params(("parallel", "parallel")),
        name="mla_attn_ctx" if has_ctx else "mla_attn",
    )(*args)


def _ctxkv_kernel(ckv_ref, kr_ref, wkv_ref, k_ref, v_ref):
    kv = _bdot(ckv_ref[0], wkv_ref[...])
    kr = kr_ref[0]
    for hd in range(MLA_HEADS):
        sl = slice(hd * SLOT, (hd + 1) * SLOT)
        k_ref[0, :, sl] = (kv[:, sl] + kr).astype(jnp.bfloat16)
    v_ref[0] = kv[:, D_SLOTS:2 * D_SLOTS].astype(jnp.bfloat16)


def _ctx_keyvals(ckv_ctx, kr_slot, wkv):
    b, p, _ = ckv_ctx.shape
    blk = lambda width: pl.BlockSpec((1, p, width), lambda bi: (bi, 0, 0))
    return pl.pallas_call(
        _ctxkv_kernel, grid=(b,),
        in_specs=[blk(KV_RANK), blk(SLOT), _const_spec(wkv.shape)],
        out_specs=[blk(D_SLOTS), blk(D_SLOTS)],
        out_shape=[jax.ShapeDtypeStruct((b, p, D_SLOTS), jnp.bfloat16)] * 2,
        compiler_params=_params(("parallel",)),
        name="ctx_keyvals",
    )(ckv_ctx, kr_slot, wkv)


def _lru_kernel(xlgl_ref, cw_ref, cb_ref, wg_ref, bg_ref, lam_ref, h0_ref, y_ref, hfin_ref,
                a_scr, b_scr, h_scr):
    l = xlgl_ref.shape[1]
    xl = xlgl_ref[0, :, 0:D_LRU]
    t_idx = lax.broadcasted_iota(jnp.int32, (l, D_LRU), 0)
    xc = cb_ref[...] + cw_ref[2:3, :] * xl
    for j, back in ((0, 2), (1, 1), (3, -1)):
        rolled = pltpu.roll(xl, back % l, 0)
        ok = (t_idx >= back) if back > 0 else (t_idx < l + back)
        xc = xc + cw_ref[j:j + 1, :] * jnp.where(ok, rolled, 0.0)
    xcb = xc.astype(jnp.bfloat16)
    r_idx = lax.broadcasted_iota(jnp.int32, (SUBLANES, D_LRU), 0)
    nblk = l // SUBLANES
    for d in range(2):
        gr = jnp.dot(xcb, wg_ref[:, d * D_LRU:(d + 1) * D_LRU], preferred_element_type=jnp.float32)
        gr = gr + bg_ref[:, d * D_LRU:(d + 1) * D_LRU]
        gi = jnp.dot(xcb, wg_ref[:, (2 + d) * D_LRU:(3 + d) * D_LRU], preferred_element_type=jnp.float32)
        gi = gi + bg_ref[:, (2 + d) * D_LRU:(3 + d) * D_LRU]
        log_a = -LRU_C * _sigmoid(gr) * _softplus(-lam_ref[d:d + 1, :])
        a = jnp.exp(log_a)
        a_scr[...] = a
        b_scr[...] = jnp.sqrt(1.0 - a * a) * _sigmoid(gi) * xc

        def block(i, carry, d=d):
            blk = i if d == 0 else nblk - 1 - i
            rs = pl.ds(pl.multiple_of(blk * SUBLANES, SUBLANES), SUBLANES)
            aa = a_scr[rs, :]
            bb = b_scr[rs, :]
            for s in (1, 2, 4):
                if d == 0:
                    keep = r_idx >= s
                    sh = s
                else:
                    keep = r_idx < SUBLANES - s
                    sh = SUBLANES - s
                a_sh = jnp.where(keep, pltpu.roll(aa, sh, 0), 1.0)
                b_sh = jnp.where(keep, pltpu.roll(bb, sh, 0), 0.0)
                bb = bb + aa * b_sh
                aa = aa * a_sh
            hh = bb + aa * carry
            h_scr[d, rs, :] = hh
            edge = SUBLANES - 1 if d == 0 else 0
            return jnp.broadcast_to(hh[edge:edge + 1, :], (SUBLANES, D_LRU))

        last = lax.fori_loop(0, nblk, block, jnp.broadcast_to(h0_ref[0, d:d + 1, :], (SUBLANES, D_LRU)))
        hfin_ref[0, d:d + 1, :] = last[0:1, :]
    gl = xlgl_ref[0, :, D_LRU:2 * D_LRU]
    y_ref[0] = (h_scr[0] + h_scr[1]) * _silu(gl)


def _lru(xlgl, h0, wts):
    b, l, _ = xlgl.shape
    return pl.pallas_call(
        _lru_kernel, grid=(b,),
        in_specs=[
            pl.BlockSpec((1, l, 2 * D_LRU), lambda bi: (bi, 0, 0)),
            _const_spec(wts["conv_w"].shape), _const_spec(wts["conv_b"].shape),
            _const_spec(wts["wg"].shape), _const_spec(wts["bg"].shape), _const_spec(wts["lam"].shape),
            pl.BlockSpec((1, 2, D_LRU), lambda bi: (bi, 0, 0)),
        ],
        out_specs=[pl.BlockSpec((1, l, D_LRU), lambda bi: (bi, 0, 0)),
                   pl.BlockSpec((1, 2, D_LRU), lambda bi: (bi, 0, 0))],
        out_shape=[jax.ShapeDtypeStruct((b, l, D_LRU), jnp.float32),
                   jax.ShapeDtypeStruct((b, 2, D_LRU), jnp.float32)],
        scratch_shapes=[pltpu.VMEM((l, D_LRU), jnp.float32), pltpu.VMEM((l, D_LRU), jnp.float32),
                        pltpu.VMEM((2, l, D_LRU), jnp.float32)],
        compiler_params=_params(("parallel",)),
        name="rglru",
    )(xlgl, wts["conv_w"], wts["conv_b"], wts["wg"], wts["bg"], wts["lam"], h0)


def _post_kernel(alpha, x_ref, mod_ref, yf_ref, yb_ref, rwe_ref, ymla_ref, ylru_ref, wo_ref, lnx_ref, ln_ref,
                 ones_ref, o_ref):
    y = jnp.concatenate(
        [jnp.concatenate([yf_ref[0, tb, hp, 0] + yb_ref[0, tb, hp, 0] for tb in range(yf_ref.shape[1])], axis=0)
         for hp in range(D_RW // LANES)], axis=-1)
    ones = ones_ref[...]
    inv_n = 1.0 / HEAD_RW
    mu = _seg_sum(y, ones) * inv_n
    yc = y - mu
    var = _seg_sum(yc * yc, ones) * inv_n
    yn = yc * lax.rsqrt(var + GN_EPS) * lnx_ref[0:1, :] + lnx_ref[1:2, :]
    y_rw = (yn + rwe_ref[0, :, D_RW:2 * D_RW]) * _silu(rwe_ref[0, :, 0:D_RW])
    out = (_bdot(y_rw, wo_ref[0:D_RW, :]) + _bdot(ymla_ref[0], wo_ref[D_RW:D_RW + D_MLA, :])
           + _bdot(ylru_ref[0], wo_ref[D_RW + D_MLA:D_RW + D_MLA + D_LRU, :]))
    z = alpha * x_ref[0] + mod_ref[0, 2:3, :] * out
    mu_z = jnp.mean(z, axis=-1, keepdims=True)
    zc = z - mu_z
    var_z = jnp.mean(zc * zc, axis=-1, keepdims=True)
    o_ref[0] = zc * lax.rsqrt(var_z + LN_EPS) * ln_ref[0:1, :] + ln_ref[1:2, :]


def _post(x, mod, ys, rwe, ymla, ylru, wts, alpha, tile):
    b, l, d = x.shape
    nt = l // tile
    mod_b = mod.shape[0]
    tok = lambda width: pl.BlockSpec((1, tile, width), lambda bi, ti: (bi, ti, 0))
    y_fwd, y_bwd, bwd_group = (ys[0], ys[1], 0) if len(ys) == 2 else (ys[0], ys[0], 1)
    y_spec = lambda grp: pl.BlockSpec((1, tile // SCAN_STEPS, 2, 1, SCAN_STEPS, LANES),
                                      lambda bi, ti: (grp, ti, 0, bi, 0, 0))
    return pl.pallas_call(
        functools.partial(_post_kernel, alpha),
        grid=(b, nt),
        in_specs=[
            tok(d),
            pl.BlockSpec((1, 3, d), (lambda bi, ti: (bi, 0, 0)) if mod_b > 1 else (lambda bi, ti: (0, 0, 0))),
            y_spec(0), y_spec(bwd_group),
            tok(2 * D_RW), tok(D_MLA), tok(D_LRU),
            _const_spec(wts["w_out"].shape), _const_spec(wts["lnx"].shape), _const_spec(wts["ln"].shape),
            _const_spec(wts["ones"].shape),
        ],
        out_specs=tok(d),
        out_shape=jax.ShapeDtypeStruct((b, l, d), jnp.float32),
        compiler_params=_params(("parallel", "parallel")),
        name="post",
    )(x, mod, y_fwd, y_bwd, rwe, ymla, ylru, wts["w_out"], wts["lnx"], wts["ln"], wts["ones"])


def _block_diag(w):
    g, i, o = w.shape
    eye = jnp.eye(g, dtype=w.dtype)
    return jnp.einsum("gio,gh->giho", w, eye).reshape(g * i, g * o)


def _place(cols, offset, width):
    d, n = cols.shape
    return jnp.pad(cols, ((0, 0), (offset, width - offset - n)))


def _pack_layer(l, P, rope):
    bf16 = jnp.bfloat16
    w_in = P["w_in"][l]
    d = w_in.shape[0]
    o_cq = 4 * D_RW
    o_ckv = o_cq + Q_RANK
    o_kr = o_ckv + KV_RANK
    o_gm = o_kr + MLA_ROPE
    o_xl = o_gm + D_MLA
    w_kr = w_in[:, o_kr:o_kr + MLA_ROPE]
    half = MLA_ROPE // 2
    perm = np.concatenate([np.arange(half) * 2, np.arange(half) * 2 + 1])
    perm_sw = np.concatenate([np.arange(half) * 2 + 1, np.arange(half) * 2])
    if rope:
        kra = _place(w_kr[:, perm], MLA_NOPE, SLOT)
        krb = _place(w_kr[:, perm_sw], MLA_NOPE, SLOT)
    else:
        kra = _place(w_kr, 0, SLOT)
        krb = _place(w_kr, MLA_NOPE, SLOT)
    lora = jnp.concatenate([P["rw_w1"][l, 0], P["rw_w1"][l, 1], P["rw_a1"][l, 0], P["rw_a1"][l, 1]], axis=1)
    w_cat = jnp.concatenate([w_in[:, 0:o_kr], kra, krb, w_in[:, o_gm:], lora], axis=1)
    assert w_cat.shape == (d, N_CAT)

    wuq = P["mla_wuq"][l].reshape(Q_RANK, MLA_HEADS, MLA_NOPE + MLA_ROPE)
    qn, qr = wuq[..., :MLA_NOPE], wuq[..., MLA_NOPE:]
    pad_q = lambda rp: jnp.pad(jnp.concatenate([qn, rp], -1), ((0, 0), (0, 0), (0, SLOT - MLA_NOPE - MLA_ROPE)))
    if rope:
        wq_a = pad_q(qr[..., perm])
        wq_b = jnp.pad(qr[..., perm_sw], ((0, 0), (0, 0), (MLA_NOPE, SLOT - MLA_NOPE - MLA_ROPE)))
        wq = jnp.concatenate([wq_a.reshape(Q_RANK, D_SLOTS), wq_b.reshape(Q_RANK, D_SLOTS)], axis=1)
    else:
        wq = pad_q(qr).reshape(Q_RANK, D_SLOTS)
    wukv = P["mla_wukv"][l].reshape(KV_RANK, MLA_HEADS, MLA_NOPE + MLA_V)
    wk = jnp.pad(wukv[..., :MLA_NOPE], ((0, 0), (0, 0), (0, SLOT - MLA_NOPE)))
    wv = wukv[..., MLA_NOPE:]
    odd = (jnp.arange(MLA_HEADS) % 2 == 1)[None, :, None]
    wv = jnp.where(odd, jnp.pad(wv, ((0, 0), (0, 0), (MLA_V, 0))), jnp.pad(wv, ((0, 0), (0, 0), (0, MLA_V))))
    wkv = jnp.concatenate([wk.reshape(KV_RANK, D_SLOTS), wv.reshape(KV_RANK, D_SLOTS)], axis=1)

    zeros_w2 = jnp.zeros_like(P["rw_w2"][l, 0])
    lw2 = jnp.block([[P["rw_w2"][l, 0], zeros_w2], [zeros_w2, P["rw_w2"][l, 1]]])
    la2 = jnp.block([[P["rw_a2"][l, 0], zeros_w2], [zeros_w2, P["rw_a2"][l, 1]]])
    rwp = jnp.concatenate([P["rw_w0"][l], P["rw_a0"][l], P["rw_kk"][l][None], P["rw_ka"][l][None],
                           P["rw_rk"][l].reshape(1, D_RW), jnp.zeros((1, D_RW), jnp.float32)], axis=0)
    grp = jnp.arange(D_RW) // HEAD_RW
    ones = (grp[:, None] == grp[None, :]).astype(bf16)
    wg = jnp.concatenate([_block_diag(P["lru_wa"][l, 0]), _block_diag(P["lru_wa"][l, 1]),
                          _block_diag(P["lru_wx"][l, 0]), _block_diag(P["lru_wx"][l, 1])], axis=1)
    bg = jnp.concatenate([P["lru_ba"][l, 0], P["lru_ba"][l, 1], P["lru_bx"][l, 0], P["lru_bx"][l, 1]])[None]
    return {
        "w_cat": w_cat.astype(bf16), "lw2": lw2.astype(bf16), "la2": la2.astype(bf16), "rwp": rwp,
        "wq": wq.astype(bf16), "wkv": wkv.astype(bf16),
        "qnorm": P["mla_qnorm"][l][None], "kvnorm": P["mla_kvnorm"][l][None], "ones": ones,
        "conv_w": P["lru_conv_w"][l], "conv_b": P["lru_conv_b"][l][None],
        "wg": wg.astype(bf16), "bg": bg, "lam": P["lru_lambda"][l],
        "w_out": P["w_out"][l].astype(bf16),
        "lnx": jnp.stack([P["rw_lnx_g"][l], P["rw_lnx_b"][l]]),
        "ln": jnp.stack([P["ln_g"][l], P["ln_b"][l]]),
        "perm": perm,
    }


def _rope_tables(l):
    rows = l // GRID_W
    row = jnp.repeat(jnp.arange(rows), GRID_W).astype(jnp.float32)
    col = jnp.tile(jnp.arange(GRID_W), rows).astype(jnp.float32)
    pairs = MLA_ROPE // 4
    inv = ROPE_BASE ** (-jnp.arange(pairs, dtype=jnp.float32) / pairs)
    ang = jnp.concatenate([row[:, None] * inv, col[:, None] * inv], axis=-1)
    cos, sin = jnp.cos(ang), jnp.sin(ang)
    tail = jnp.zeros((l, SLOT - MLA_NOPE - MLA_ROPE), jnp.float32)
    nope0 = jnp.zeros((l, MLA_NOPE), jnp.float32)
    ca = jnp.concatenate([nope0, cos, cos, tail], axis=1)
    sb = jnp.concatenate([nope0, -sin, sin, tail], axis=1)
    ta = jnp.concatenate([jnp.ones((l, MLA_NOPE), jnp.float32), cos, cos, tail], axis=1) * SOFTMAX_SCALE
    tb = sb * SOFTMAX_SCALE
    return ta, tb, ca, sb


def _layer(x, mod, wts, alpha, ctx, rope_tabs):
    b, l, _ = x.shape
    tile = min(l, 256)
    outs = _pre(x, mod, wts, rope_tabs, tile)
    rwk, rwe, q, k, v, gm, xlgl = outs[:7]
    if ctx is None:
        s0 = jnp.zeros((b, 2, N_HEAD_RW, HEAD_RW, HEAD_RW), jnp.float32)
        h0 = jnp.zeros((b, 2, D_LRU), jnp.float32)
        ctx_kv = None
    else:
        s0, ckv_ctx, kr_ctx, h0 = ctx
        kr_slot = jnp.pad(kr_ctx[..., wts["perm"]], ((0, 0), (0, 0), (MLA_NOPE, SLOT - MLA_NOPE - MLA_ROPE)))
        ctx_kv = _ctx_keyvals(ckv_ctx, kr_slot, wts["wkv"])
    yrw, s_fin = _rwkv(rwk, s0)
    ymla = _attention(q, k, v, gm, ctx_kv, tile)
    ylru, h_fin = _lru(xlgl, h0, wts)
    x_new = _post(x, mod, yrw, rwe, ymla, ylru, wts, alpha, tile)
    extras = (s_fin, outs[7], outs[8], h_fin) if ctx is None else None
    return x_new, extras


def kernel(x_prompt, x_sample, state_rwkv, cache_mla_ckv, cache_mla_krope, state_lru, c, c_ctx, w_mod, b_mod,
           w_in, rw_w0, rw_w1, rw_w2, rw_a0, rw_a1, rw_a2, rw_kk, rw_ka, rw_rk, rw_lnx_g, rw_lnx_b, mla_qnorm,
           mla_wuq, mla_kvnorm, mla_wukv, lru_conv_w, lru_conv_b, lru_wa, lru_ba, lru_wx, lru_bx, lru_lambda,
           w_out, ln_g, ln_b):
    P = dict(w_in=w_in, rw_w0=rw_w0, rw_w1=rw_w1, rw_w2=rw_w2, rw_a0=rw_a0, rw_a1=rw_a1, rw_a2=rw_a2,
             rw_kk=rw_kk, rw_ka=rw_ka, rw_rk=rw_rk, rw_lnx_g=rw_lnx_g, rw_lnx_b=rw_lnx_b, mla_qnorm=mla_qnorm,
             mla_wuq=mla_wuq, mla_kvnorm=mla_kvnorm, mla_wukv=mla_wukv, lru_conv_w=lru_conv_w,
             lru_conv_b=lru_conv_b, lru_wa=lru_wa, lru_ba=lru_ba, lru_wx=lru_wx, lru_bx=lru_bx,
             lru_lambda=lru_lambda, w_out=w_out, ln_g=ln_g, ln_b=ln_b)
    depth = w_in.shape[0]
    alpha = (2 * depth) ** 0.25
    dec_b, dec_l, d = x_sample.shape

    rows = -(-(1 + dec_b) // SUBLANES) * SUBLANES
    cvecs = jnp.concatenate([c_ctx[None, :], c, jnp.zeros((rows - 1 - dec_b, d), jnp.float32)], axis=0)
    mod_all = _modulation(cvecs, w_mod, b_mod).reshape(depth, rows, 3, d)
    rope_tabs = _rope_tables(dec_l)

    xp, xs = x_prompt, x_sample
    rw_states, ckvs, krs, lru_states = [], [], [], []
    for l in range(depth):
        xp, (s_l, ckv_l, kr_l, h_l) = _layer(xp, mod_all[l, 0:1], _pack_layer(l, P, rope=False), alpha, None, None)
        rw_states.append(s_l)
        ckvs.append(ckv_l)
        krs.append(kr_l)
        lru_states.append(h_l)
    for l in range(depth):
        ctx = (state_rwkv[:, l], cache_mla_ckv[:, l], cache_mla_krope[:, l], state_lru[:, l])
        xs, _ = _layer(xs, mod_all[l, 1:1 + dec_b], _pack_layer(l, P, rope=True), alpha, ctx, rope_tabs)
    return (xp, xs, jnp.stack(rw_states, axis=1), jnp.stack(ckvs, axis=1), jnp.stack(krs, axis=1),
            jnp.stack(lru_states, axis=1))
```

```python
import functools
import math

import jax
import jax.numpy as jnp
import numpy as np
from jax import lax
from jax.experimental import pallas as pl
from jax.experimental.pallas import tpu as pltpu

LANES = 128
SUBLANES = 8
VMEM_LIMIT_BYTES = 56 * 1024 * 1024

N_HEAD_RW = 4
HEAD_RW = 64
D_RW = N_HEAD_RW * HEAD_RW
MLA_HEADS = 8
MLA_NOPE = 64
MLA_ROPE = 32
MLA_V = 64
D_MLA = MLA_HEADS * MLA_V
Q_RANK = 256
KV_RANK = 128
D_LRU = 256
LRU_BLOCKS = 4
LRU_BLOCK = 64
GRID_W = 64
ROPE_BASE = 10000.0
SOFTMAX_SCALE = (MLA_NOPE + MLA_ROPE) ** -0.5
Q_SCALE = SOFTMAX_SCALE * math.log2(math.e)
LRU_C = 8.0
GN_EPS = 64e-5
LN_EPS = 1e-5
RMS_EPS = 1e-6
SLOT = LANES
D_SLOTS = MLA_HEADS * SLOT

C_RKVG = 0
C_CQ = C_RKVG + 4 * D_RW
C_CKV = C_CQ + Q_RANK
C_KRA = C_CKV + KV_RANK
C_KRB = C_KRA + SLOT
C_GM = C_KRB + SLOT
C_XL = C_GM + D_MLA
C_GL = C_XL + D_LRU
C_LORA = C_GL + D_LRU
N_CAT = C_LORA + 4 * 64

R_W0, R_A0, R_KK, R_KA, R_RK = 0, 2, 4, 5, 6

SCAN_STEPS = 16
RWK_R, RWK_NKK, RWK_V, RWK_DIR = 0, 1, 2, 3
N_RWK_CHUNKS = 2 * 9
RWK_SHARED_CHUNKS = 2 * 3


def _params(sem):
    return pltpu.CompilerParams(dimension_semantics=sem, vmem_limit_bytes=VMEM_LIMIT_BYTES)


def _const_spec(shape):
    zeros = (0,) * len(shape)
    return pl.BlockSpec(shape, lambda *_: zeros)


def _split3(x):
    hi = x.astype(jnp.bfloat16)
    r1 = x - hi.astype(jnp.float32)
    mid = r1.astype(jnp.bfloat16)
    lo = (r1 - mid.astype(jnp.float32)).astype(jnp.bfloat16)
    return hi, mid, lo


def _seg_sum(x, ones_bf16):
    hi, mid, lo = _split3(x)
    dot = functools.partial(jnp.dot, preferred_element_type=jnp.float32)
    return dot(hi, ones_bf16) + dot(mid, ones_bf16) + dot(lo, ones_bf16)


def _sigmoid(x):
    return 1.0 / (1.0 + jnp.exp(-x))


def _silu(x):
    return x * _sigmoid(x)


def _softplus(x):
    return jnp.maximum(x, 0.0) + jnp.log1p(jnp.exp(-jnp.abs(x)))


def _bdot(a, b):
    return jnp.dot(a.astype(jnp.bfloat16), b, preferred_element_type=jnp.float32)


_CONTRACT_LAST = (((1,), (1,)), ((), ()))


def _values_t(wv_t, ckvn):
    return lax.dot_general(wv_t, ckvn.astype(jnp.bfloat16), _CONTRACT_LAST,
                           preferred_element_type=jnp.float32).astype(jnp.bfloat16)


def _mod_kernel(c_ref, w_ref, b_ref, o_ref):
    a = _silu(c_ref[...])
    w = w_ref[0]
    a_hi, a_mid, _ = _split3(a)
    w_hi, w_mid, _ = _split3(w)
    dot = functools.partial(jnp.dot, preferred_element_type=jnp.float32)
    o_ref[0] = dot(a_hi, w_hi) + dot(a_hi, w_mid) + dot(a_mid, w_hi) + b_ref[0]


def _modulation(cvecs, w_mod, b_mod):
    depth, d, d3 = w_mod.shape
    rows = cvecs.shape[0]
    nblk = d3 // d
    return pl.pallas_call(
        _mod_kernel,
        grid=(depth, nblk),
        in_specs=[
            pl.BlockSpec((rows, d), lambda l, j: (0, 0)),
            pl.BlockSpec((1, d, d), lambda l, j: (l, 0, j)),
            pl.BlockSpec((1, 1, d), lambda l, j: (l, 0, j)),
        ],
        out_specs=pl.BlockSpec((1, rows, d), lambda l, j: (l, 0, j)),
        out_shape=jax.ShapeDtypeStruct((depth, rows, d3), jnp.float32),
        compiler_params=_params(("parallel", "parallel")),
        name="adaln_mod",
    )(cvecs, w_mod, b_mod.reshape(depth, 1, d3))


def _pre_kernel(rope, *refs):
    if rope:
        (x_ref, mod_ref, win_ref, lw2_ref, la2_ref, rwp_ref, wq_ref, wkv_ref, wvt_ref, qn_ref, kvn_ref,
         ones_ref, ta_ref, tb_ref, ca_ref, sb_ref,
         rwk_ref, rwe_ref, q_ref, k_ref, v_ref, gm_ref, xlgl_ref) = refs
    else:
        (x_ref, mod_ref, win_ref, lw2_ref, la2_ref, rwp_ref, wq_ref, wkv_ref, wvt_ref, qn_ref, kvn_ref,
         ones_ref,
         rwk_ref, rwe_ref, q_ref, k_ref, v_ref, gm_ref, xlgl_ref, ckvn_ref, kr_ref) = refs

    x = x_ref[0]
    shift = mod_ref[0, 0:1, :]
    scale = mod_ref[0, 1:2, :]
    mu = jnp.mean(x, axis=-1, keepdims=True)
    xc = x - mu
    var = jnp.mean(xc * xc, axis=-1, keepdims=True)
    u = xc * lax.rsqrt(var + LN_EPS) * (1.0 + scale) + shift
    h = _bdot(u, win_ref[...])

    r = h[:, 0:D_RW]
    k = h[:, D_RW:2 * D_RW]
    v = h[:, 2 * D_RW:3 * D_RW]
    g = h[:, 3 * D_RW:4 * D_RW]
    lora_w = h[:, C_LORA:C_LORA + 128]
    lora_a = h[:, C_LORA + 128:C_LORA + 256]
    wl = _bdot(jnp.tanh(lora_w), lw2_ref[...])
    al = _bdot(lora_a, la2_ref[...])
    ones = ones_ref[...]
    kk = k * rwp_ref[R_KK:R_KK + 1, :]
    kk = kk / jnp.maximum(jnp.sqrt(_seg_sum(kk * kk, ones)), 1e-12)
    bonus = _seg_sum(r * k * rwp_ref[R_RK:R_RK + 1, :], ones) * v
    ka = rwp_ref[R_KA:R_KA + 1, :]
    def put(slot, val):
        for tb in range(val.shape[0] // SCAN_STEPS):
            for half in range(D_RW // LANES):
                rwk_ref[tb, 2 * slot + half, 0] = val[tb * SCAN_STEPS:(tb + 1) * SCAN_STEPS,
                                                      half * LANES:(half + 1) * LANES]

    put(RWK_R, r)
    put(RWK_NKK, -kk)
    put(RWK_V, v)
    for d in range(2):
        wl_d = wl[:, d * D_RW:(d + 1) * D_RW] + rwp_ref[R_W0 + d:R_W0 + d + 1, :]
        decay = jnp.exp(-math.exp(-0.5) * _sigmoid(wl_d))
        a_d = _sigmoid(al[:, d * D_RW:(d + 1) * D_RW] + rwp_ref[R_A0 + d:R_A0 + d + 1, :])
        put(RWK_DIR + 3 * d, decay)
        put(RWK_DIR + 3 * d + 1, kk * a_d)
        put(RWK_DIR + 3 * d + 2, k * (1.0 + (a_d - 1.0) * ka))
    rwe_ref[0, :, 0:D_RW] = g
    rwe_ref[0, :, D_RW:2 * D_RW] = bonus

    cq = h[:, C_CQ:C_CQ + Q_RANK]
    cqn = cq * lax.rsqrt(jnp.mean(cq * cq, axis=-1, keepdims=True) + RMS_EPS) * qn_ref[...]
    ckv = h[:, C_CKV:C_CKV + KV_RANK]
    ckvn = ckv * lax.rsqrt(jnp.mean(ckv * ckv, axis=-1, keepdims=True) + RMS_EPS) * kvn_ref[...]
    qq = _bdot(cqn, wq_ref[...])
    kv = _bdot(ckvn, wkv_ref[...])
    v_ref[0] = _values_t(wvt_ref[...], ckvn)
    kra = h[:, C_KRA:C_KRA + SLOT]
    krb = h[:, C_KRB:C_KRB + SLOT]
    if rope:
        k_rope = kra * ca_ref[...] + krb * sb_ref[...]
        ta = ta_ref[...]
        tb = tb_ref[...]
    else:
        k_rope = krb
        ckvn_ref[0] = ckvn
        kr_ref[0] = kra[:, 0:MLA_ROPE]
    for hd in range(MLA_HEADS):
        lo, hi = hd * SLOT, (hd + 1) * SLOT
        if rope:
            qh = qq[:, lo:hi] * ta + qq[:, D_SLOTS + lo:D_SLOTS + hi] * tb
        else:
            qh = qq[:, lo:hi] * Q_SCALE
        q_ref[0, :, lo:hi] = qh.astype(jnp.bfloat16)
        k_ref[0, :, lo:hi] = (kv[:, lo:hi] + k_rope).astype(jnp.bfloat16)
    gm_ref[0] = h[:, C_GM:C_GM + D_MLA]
    xlgl_ref[0] = h[:, C_XL:C_XL + 2 * D_LRU]


def _pre(x, mod, wts, rope_tabs, tile):
    b, l, d = x.shape
    rope = rope_tabs is not None
    nt = l // tile
    mod_b = mod.shape[0]
    tok = lambda width: pl.BlockSpec((1, tile, width), lambda bi, ti: (bi, ti, 0))
    in_specs = [
        tok(d),
        pl.BlockSpec((1, 3, d), (lambda bi, ti: (bi, 0, 0)) if mod_b > 1 else (lambda bi, ti: (0, 0, 0))),
        _const_spec(wts["w_cat"].shape), _const_spec(wts["lw2"].shape), _const_spec(wts["la2"].shape),
        _const_spec(wts["rwp"].shape), _const_spec(wts["wq"].shape), _const_spec(wts["wkv"].shape),
        _const_spec(wts["wv_t"].shape), _const_spec(wts["qnorm"].shape), _const_spec(wts["kvnorm"].shape), _const_spec(wts["ones"].shape),
    ]
    args = [x, mod, wts["w_cat"], wts["lw2"], wts["la2"], wts["rwp"], wts["wq"], wts["wkv"], wts["wv_t"],
            wts["qnorm"], wts["kvnorm"], wts["ones"]]
    f32, bf16 = jnp.float32, jnp.bfloat16
    out_shape = [
        jax.ShapeDtypeStruct((l // SCAN_STEPS, N_RWK_CHUNKS, b, SCAN_STEPS, LANES), f32),
        jax.ShapeDtypeStruct((b, l, 2 * D_RW), f32),
        jax.ShapeDtypeStruct((b, l, D_SLOTS), bf16), jax.ShapeDtypeStruct((b, l, D_SLOTS), bf16),
        jax.ShapeDtypeStruct((b, D_MLA, l), bf16), jax.ShapeDtypeStruct((b, l, D_MLA), f32),
        jax.ShapeDtypeStruct((b, l, 2 * D_LRU), f32),
    ]
    rwk_spec = pl.BlockSpec((tile // SCAN_STEPS, N_RWK_CHUNKS, 1, SCAN_STEPS, LANES),
                            lambda bi, ti: (ti, 0, bi, 0, 0))
    vt_spec = pl.BlockSpec((1, D_MLA, tile), lambda bi, ti: (bi, 0, ti))
    out_specs = [rwk_spec, tok(2 * D_RW), tok(D_SLOTS), tok(D_SLOTS), vt_spec, tok(D_MLA), tok(2 * D_LRU)]
    if rope:
        tab = pl.BlockSpec((tile, SLOT), lambda bi, ti: (ti, 0))
        in_specs += [tab, tab, tab, tab]
        args += list(rope_tabs)
    else:
        out_shape += [jax.ShapeDtypeStruct((b, l, KV_RANK), f32), jax.ShapeDtypeStruct((b, l, MLA_ROPE), f32)]
        out_specs += [tok(KV_RANK), tok(MLA_ROPE)]
    return pl.pallas_call(
        functools.partial(_pre_kernel, rope),
        grid=(b, nt), in_specs=in_specs, out_specs=out_specs, out_shape=out_shape,
        compiler_params=_params(("parallel", "parallel")),
        name="pre_rope" if rope else "pre_ctx",
    )(*args)


KT_R, KT_W, KT_B, KT_KD = range(4)
N_KT = 4
PLANES = 2


def _scan_kernel(dl, vs, steps, *refs):
    sh_refs, dr_refs = refs[0:dl], refs[dl:2 * dl]
    s0_ref = refs[2 * dl]
    y_refs = refs[2 * dl + 1:3 * dl + 1]
    sfin_ref = refs[3 * dl + 1]
    s_scr = refs[3 * dl + 2]
    kt_scrs, vv_scrs, nkk_scrs = refs[3 * dl + 3:3 * dl + 5], refs[3 * dl + 5:3 * dl + 7], refs[3 * dl + 7:]
    g = pl.program_id(0)
    i = pl.program_id(1)
    rows_per_chunk = sh_refs[0].shape[1] // RWK_SHARED_CHUNKS
    batch = rows_per_chunk // steps
    vr = HEAD_RW // vs
    lane_split = (lax.broadcasted_iota(jnp.int32, (vr, LANES), 1) // (2 * batch)) % vs

    def block_row(e, j):
        if dl == 2:
            return j if e == 0 else steps - 1 - j
        return j + g * (steps - 1 - 2 * j)

    def batch_rows(ref, chunk, t):
        return ref[0, pl.ds(chunk * rows_per_chunk + t, batch, stride=steps), :]

    def gather(in_refs, slot, j):
        parts = []
        for e in range(dl):
            t = block_row(e, j)
            pair = [batch_rows(in_refs[e], 2 * slot + hp, t) for hp in range(2)]
            parts += pair * vs
        return jnp.concatenate(parts, axis=0)

    @pl.when(i == 0)
    def _():
        s_scr[...] = s0_ref[0]

    def prepare_tile(kt, tile, in_refs, slot, j):
        kt[tile] = gather(in_refs, slot, j).T

    def prepare_v(j, buf):
        v_t = gather(sh_refs, RWK_V, j).T
        for p in range(PLANES):
            base = p * HEAD_RW
            vv = v_t[base:base + vr, :]
            for s in range(1, vs):
                vv = jnp.where(lane_split == s, v_t[base + s * vr:base + (s + 1) * vr, :], vv)
            vv_scrs[buf][p] = vv

    def prepare_nkk(j, buf):
        nkk_scrs[buf][...] = gather(sh_refs, RWK_NKK, jnp.minimum(j, steps - 1)).T

    def row(ref, *idx):
        return jnp.broadcast_to(ref[idx[:-1] + (pl.ds(idx[-1], 1), slice(None))], (vr, LANES))

    def tree_sum(terms):
        while len(terms) > 1:
            terms = [terms[a] + terms[a + 1] for a in range(0, len(terms), 2)]
        return terms[0]

    def emit(j, y_planes):
        zero = jnp.zeros((vr, LANES), jnp.float32)
        y_full = jnp.concatenate([jnp.where(lane_split == s, y_planes[p], zero)
                                  for p in range(PLANES) for s in range(vs)], axis=0)
        y_t = y_full.T
        for e in range(dl):
            t = block_row(e, j)
            for hp in range(2):
                parts = [y_t[((e * vs + s) * 2 + hp) * batch:((e * vs + s) * 2 + hp + 1) * batch, :]
                         for s in range(vs)]
                y_refs[e][0, 0, pl.ds(hp * rows_per_chunk + t, batch, stride=steps), :] = tree_sum(parts)

    for tile, in_refs, slot in ((KT_R, sh_refs, RWK_R), (KT_W, dr_refs, 0), (KT_B, dr_refs, 1), (KT_KD, dr_refs, 2)):
        prepare_tile(kt_scrs[0], tile, in_refs, slot, 0)
    prepare_v(0, 0)
    prepare_nkk(0, 0)
    prepare_nkk(1, 1)
    sa_first = tuple(
        tree_sum([tree_sum([s_scr[p, k] * row(nkk_scrs[0], p * HEAD_RW + k) for k in range(k0, HEAD_RW, 4)])
                  for k0 in range(4)])
        for p in range(PLANES))

    def step(j, cur, sa, y_prev):
        nxt = 1 - cur
        kt, vv_ref, nkk_next = kt_scrs[cur], vv_scrs[cur], nkk_scrs[nxt]
        j1 = jnp.minimum(j + 1, steps - 1)
        side_work = [
            lambda: emit(jnp.maximum(j - 1, 0), y_prev),
            lambda: prepare_tile(kt_scrs[nxt], KT_W, dr_refs, 0, j1),
            lambda: prepare_tile(kt_scrs[nxt], KT_B, dr_refs, 1, j1),
            lambda: prepare_tile(kt_scrs[nxt], KT_KD, dr_refs, 2, j1),
            lambda: prepare_tile(kt_scrs[nxt], KT_R, sh_refs, RWK_R, j1),
            lambda: prepare_v(j1, nxt),
            lambda: prepare_nkk(j + 2, cur),
        ]
        chunk = PLANES * HEAD_RW // (len(side_work) + 1)
        y_planes, sa_next = [], []
        done = 0
        for p in range(PLANES):
            base = p * HEAD_RW
            vv = vv_ref[p]
            y_acc, sa_acc = [None, None], [None, None]
            for k in range(HEAD_RW):
                s_new = s_scr[p, k] * row(kt, KT_W, base + k) + (
                    sa[p] * row(kt, KT_B, base + k) + vv * row(kt, KT_KD, base + k))
                s_scr[p, k] = s_new
                y_term = s_new * row(kt, KT_R, base + k)
                sa_term = s_new * row(nkk_next, base + k)
                h = k % 2
                y_acc[h] = y_term if y_acc[h] is None else y_acc[h] + y_term
                sa_acc[h] = sa_term if sa_acc[h] is None else sa_acc[h] + sa_term
                done += 1
                if done % chunk == 0 and side_work:
                    side_work.pop(0)()
            y_planes.append(y_acc[0] + y_acc[1])
            sa_next.append(sa_acc[0] + sa_acc[1])
        return tuple(sa_next), tuple(y_planes)

    def two_steps(jj, carry):
        for cur in range(2):
            carry = step(2 * jj + cur, cur, *carry)
        return carry

    zeros = tuple(jnp.zeros((vr, LANES), jnp.float32) for _ in range(PLANES))
    _, y_last = lax.fori_loop(0, steps // 2, two_steps, (sa_first, zeros))
    emit(steps - 1, y_last)

    @pl.when(i == pl.num_programs(1) - 1)
    def _():
        sfin_ref[0] = s_scr[...]


def _lane_plan(batch):
    per_dir = (N_HEAD_RW // PLANES) * batch
    assert LANES % per_dir == 0
    fill = LANES // per_dir
    dl = 2 if fill >= 4 else 1
    vs = fill // dl
    assert dl * vs * per_dir == LANES and HEAD_RW % (vs * SUBLANES) == 0
    return 2 // dl, dl, vs


def _rwkv_scan(rwk, s0_lanes, plan):
    groups, dl, vs = plan
    nt, chunks, b, steps, _ = rwk.shape
    vr = HEAD_RW // vs
    rows = b * steps
    rwk_rows = rwk.reshape(nt, chunks * rows, LANES)

    def time_block(d, ti):
        return ti + d * (nt - 1 - 2 * ti)

    def slot_dir(e, gi):
        return e if dl == 2 else gi

    blk = (1, RWK_SHARED_CHUNKS * rows, LANES)
    in_specs = [pl.BlockSpec(blk, functools.partial(lambda e, gi, ti: (time_block(slot_dir(e, gi), ti), 0, 0), e))
                for e in range(dl)]
    in_specs += [pl.BlockSpec(blk, functools.partial(
        lambda e, gi, ti: (time_block(slot_dir(e, gi), ti), 1 + slot_dir(e, gi), 0), e)) for e in range(dl)]
    state_spec = pl.BlockSpec((1, PLANES, HEAD_RW, vr, LANES), lambda gi, ti: (gi, 0, 0, 0, 0))
    in_specs.append(state_spec)
    y_specs = [pl.BlockSpec((1, 1, 2 * rows, LANES), functools.partial(
        lambda e, gi, ti: (gi, time_block(slot_dir(e, gi), ti), 0, 0), e)) for e in range(dl)]
    outs = pl.pallas_call(
        functools.partial(_scan_kernel, dl, vs, steps),
        grid=(groups, nt),
        in_specs=in_specs,
        out_specs=y_specs + [state_spec],
        out_shape=[jax.ShapeDtypeStruct((groups, nt, 2 * rows, LANES), jnp.float32)] * dl
        + [jax.ShapeDtypeStruct((groups, PLANES, HEAD_RW, vr, LANES), jnp.float32)],
        scratch_shapes=[pltpu.VMEM((PLANES, HEAD_RW, vr, LANES), jnp.float32),
                        pltpu.VMEM((N_KT, LANES, LANES), jnp.float32), pltpu.VMEM((N_KT, LANES, LANES), jnp.float32),
                        pltpu.VMEM((PLANES, vr, LANES), jnp.float32), pltpu.VMEM((PLANES, vr, LANES), jnp.float32),
                        pltpu.VMEM((LANES, LANES), jnp.float32), pltpu.VMEM((LANES, LANES), jnp.float32)],
        compiler_params=_params(("parallel", "arbitrary")),
        name="rwkv_scan",
    )(*([rwk_rows] * (2 * dl)), s0_lanes)
    ys = [y.reshape(groups, nt, 2, b, steps, LANES) for y in outs[:dl]]
    return ys, outs[dl]


def _state_to_lanes(s, plan):
    g, dl, vs = plan
    b = s.shape[0]
    vr = HEAD_RW // vs
    t = s.reshape(b, g, dl, N_HEAD_RW // PLANES, PLANES, vs, vr, HEAD_RW)
    t = jnp.transpose(t, (1, 4, 7, 6, 2, 5, 3, 0))
    return t.reshape(g, PLANES, HEAD_RW, vr, LANES)


def _state_from_lanes(s, plan, b):
    g, dl, vs = plan
    vr = HEAD_RW // vs
    t = s.reshape(g, PLANES, HEAD_RW, vr, dl, vs, N_HEAD_RW // PLANES, b)
    t = jnp.transpose(t, (7, 0, 4, 6, 1, 5, 3, 2))
    return t.reshape(b, 2, N_HEAD_RW, HEAD_RW, HEAD_RW)


def _rwkv(rwk, s0):
    b = rwk.shape[2]
    plan = _lane_plan(b)
    ys, s_fin = _rwkv_scan(rwk, _state_to_lanes(s0, plan), plan)
    return ys, _state_from_lanes(s_fin, plan, b)


ROW_GROUP = 64


def _reduce_rows(x, op):
    r, c = x.shape
    if r > ROW_GROUP and r % ROW_GROUP == 0:
        x = op(x.reshape(r // ROW_GROUP, ROW_GROUP, c), axis=0)
    return op(x, axis=0, keepdims=True)


def _attn_kernel(has_ctx, *refs):
    if has_ctx:
        q_ref, k_ref, v_ref, kc_ref, vc_ref, gm_ref, o_ref = refs
    else:
        q_ref, k_ref, v_ref, gm_ref, o_ref = refs
    def scores(hd):
        sl = slice(hd * SLOT, (hd + 1) * SLOT)
        q = q_ref[0, :, sl]
        s = lax.dot_general(k_ref[0, :, sl], q, _CONTRACT_LAST, preferred_element_type=jnp.float32)
        sc = (lax.dot_general(kc_ref[0, :, sl], q, _CONTRACT_LAST, preferred_element_type=jnp.float32)
              if has_ctx else None)
        return s, sc

    nxt = scores(0)
    outs = []
    for hd in range(MLA_HEADS):
        s, sc = nxt
        if hd + 1 < MLA_HEADS:
            nxt = scores(hd + 1)
        vrows = slice(hd * MLA_V, (hd + 1) * MLA_V)
        m = _reduce_rows(s, jnp.max)
        if has_ctx:
            m = jnp.maximum(m, _reduce_rows(sc, jnp.max))
        p = jnp.exp2(s - m)
        den = _reduce_rows(p, jnp.sum)
        o = jnp.dot(v_ref[0, vrows, :], p.astype(jnp.bfloat16), preferred_element_type=jnp.float32)
        if has_ctx:
            pc = jnp.exp2(sc - m)
            den = den + _reduce_rows(pc, jnp.sum)
            o = o + jnp.dot(vc_ref[0, vrows, :], pc.astype(jnp.bfloat16), preferred_element_type=jnp.float32)
        outs.append(o * (1.0 / den))
        if hd % 2 == 1:
            cols = slice((hd // 2) * LANES, (hd // 2 + 1) * LANES)
            o_ref[0, :, cols] = jnp.concatenate(outs, axis=0).T * _silu(gm_ref[0, :, cols])
            outs = []


def _attention(q, k, v, gm, ctx_kv, tile):
    b, l, _ = q.shape
    nt = l // tile
    has_ctx = ctx_kv is not None
    tok = lambda width: pl.BlockSpec((1, tile, width), lambda bi, ti: (bi, ti, 0))
    full = lambda arr: pl.BlockSpec((1,) + arr.shape[1:], lambda bi, ti: (bi, 0, 0))
    in_specs = [tok(D_SLOTS), full(k), full(v)]
    args = [q, k, v]
    if has_ctx:
        in_specs += [full(ctx_kv[0]), full(ctx_kv[1])]
        args += list(ctx_kv)
    in_specs.append(tok(D_MLA))
    args.append(gm)
    return pl.pallas_call(
        functools.partial(_attn_kernel, has_ctx),
        grid=(b, nt), in_specs=in_specs, out_specs=tok(D_MLA),
        out_shape=jax.ShapeDtypeStruct((b, l, D_MLA), jnp.float32),
        compiler_params=_params(("parallel", "parallel")),
        name="mla_attn_ctx" if has_ctx else "mla_attn",
    )(*args)


def _ctxkv_kernel(ckv_ref, kr_ref, wkv_ref, wvt_ref, k_ref, v_ref):
    kv = _bdot(ckv_ref[0], wkv_ref[...])
    kr = kr_ref[0]
    for hd in range(MLA_HEADS):
        sl = slice(hd * SLOT, (hd + 1) * SLOT)
        k_ref[0, :, sl] = (kv[:, sl] + kr).astype(jnp.bfloat16)
    v_ref[0] = _values_t(wvt_ref[...], ckv_ref[0])


def _ctx_keyvals(ckv_ctx, kr_slot, wkv, wv_t):
    b, p, _ = ckv_ctx.shape
    blk = lambda width: pl.BlockSpec((1, p, width), lambda bi: (bi, 0, 0))
    return pl.pallas_call(
        _ctxkv_kernel, grid=(b,),
        in_specs=[blk(KV_RANK), blk(SLOT), _const_spec(wkv.shape), _const_spec(wv_t.shape)],
        out_specs=[blk(D_SLOTS), pl.BlockSpec((1, D_MLA, p), lambda bi: (bi, 0, 0))],
        out_shape=[jax.ShapeDtypeStruct((b, p, D_SLOTS), jnp.bfloat16),
                   jax.ShapeDtypeStruct((b, D_MLA, p), jnp.bfloat16)],
        compiler_params=_params(("parallel",)),
        name="ctx_keyvals",
    )(ckv_ctx, kr_slot, wkv, wv_t)


def _lru_kernel(xlgl_ref, cw_ref, cb_ref, wg_ref, bg_ref, lam_ref, h0_ref, y_ref, hfin_ref,
                a_scr, b_scr, h_scr):
    l = xlgl_ref.shape[1]
    xl = xlgl_ref[0, :, 0:D_LRU]
    t_idx = lax.broadcasted_iota(jnp.int32, (l, D_LRU), 0)
    xc = cb_ref[...] + cw_ref[2:3, :] * xl
    for j, back in ((0, 2), (1, 1), (3, -1)):
        rolled = pltpu.roll(xl, back % l, 0)
        ok = (t_idx >= back) if back > 0 else (t_idx < l + back)
        xc = xc + cw_ref[j:j + 1, :] * jnp.where(ok, rolled, 0.0)
    xcb = xc.astype(jnp.bfloat16)
    r_idx = lax.broadcasted_iota(jnp.int32, (SUBLANES, D_LRU), 0)
    nblk = l // SUBLANES
    for d in range(2):
        gr = jnp.dot(xcb, wg_ref[:, d * D_LRU:(d + 1) * D_LRU], preferred_element_type=jnp.float32)
        gr = gr + bg_ref[:, d * D_LRU:(d + 1) * D_LRU]
        gi = jnp.dot(xcb, wg_ref[:, (2 + d) * D_LRU:(3 + d) * D_LRU], preferred_element_type=jnp.float32)
        gi = gi + bg_ref[:, (2 + d) * D_LRU:(3 + d) * D_LRU]
        log_a = -LRU_C * _sigmoid(gr) * _softplus(-lam_ref[d:d + 1, :])
        a = jnp.exp(log_a)
        a_scr[...] = a
        b_scr[...] = jnp.sqrt(1.0 - a * a) * _sigmoid(gi) * xc

        def block(i, carry, d=d):
            blk = i if d == 0 else nblk - 1 - i
            rs = pl.ds(pl.multiple_of(blk * SUBLANES, SUBLANES), SUBLANES)
            aa = a_scr[rs, :]
            bb = b_scr[rs, :]
            for s in (1, 2, 4):
                if d == 0:
                    keep = r_idx >= s
                    sh = s
                else:
                    keep = r_idx < SUBLANES - s
                    sh = SUBLANES - s
                a_sh = jnp.where(keep, pltpu.roll(aa, sh, 0), 1.0)
                b_sh = jnp.where(keep, pltpu.roll(bb, sh, 0), 0.0)
                bb = bb + aa * b_sh
                aa = aa * a_sh
            hh = bb + aa * carry
            h_scr[d, rs, :] = hh
            edge = SUBLANES - 1 if d == 0 else 0
            return jnp.broadcast_to(hh[edge:edge + 1, :], (SUBLANES, D_LRU))

        last = lax.fori_loop(0, nblk, block, jnp.broadcast_to(h0_ref[0, d:d + 1, :], (SUBLANES, D_LRU)))
        hfin_ref[0, d:d + 1, :] = last[0:1, :]
    gl = xlgl_ref[0, :, D_LRU:2 * D_LRU]
    y_ref[0] = (h_scr[0] + h_scr[1]) * _silu(gl)


def _lru(xlgl, h0, wts):
    b, l, _ = xlgl.shape
    return pl.pallas_call(
        _lru_kernel, grid=(b,),
        in_specs=[
            pl.BlockSpec((1, l, 2 * D_LRU), lambda bi: (bi, 0, 0)),
            _const_spec(wts["conv_w"].shape), _const_spec(wts["conv_b"].shape),
            _const_spec(wts["wg"].shape), _const_spec(wts["bg"].shape), _const_spec(wts["lam"].shape),
            pl.BlockSpec((1, 2, D_LRU), lambda bi: (bi, 0, 0)),
        ],
        out_specs=[pl.BlockSpec((1, l, D_LRU), lambda bi: (bi, 0, 0)),
                   pl.BlockSpec((1, 2, D_LRU), lambda bi: (bi, 0, 0))],
        out_shape=[jax.ShapeDtypeStruct((b, l, D_LRU), jnp.float32),
                   jax.ShapeDtypeStruct((b, 2, D_LRU), jnp.float32)],
        scratch_shapes=[pltpu.VMEM((l, D_LRU), jnp.float32), pltpu.VMEM((l, D_LRU), jnp.float32),
                        pltpu.VMEM((2, l, D_LRU), jnp.float32)],
        compiler_params=_params(("parallel",)),
        name="rglru",
    )(xlgl, wts["conv_w"], wts["conv_b"], wts["wg"], wts["bg"], wts["lam"], h0)


def _post_kernel(alpha, x_ref, mod_ref, yf_ref, yb_ref, rwe_ref, ymla_ref, ylru_ref, wo_ref, lnx_ref, ln_ref,
                 ones_ref, o_ref):
    y = jnp.concatenate(
        [jnp.concatenate([yf_ref[0, tb, hp, 0] + yb_ref[0, tb, hp, 0] for tb in range(yf_ref.shape[1])], axis=0)
         for hp in range(D_RW // LANES)], axis=-1)
    ones = ones_ref[...]
    inv_n = 1.0 / HEAD_RW
    mu = _seg_sum(y, ones) * inv_n
    yc = y - mu
    var = _seg_sum(yc * yc, ones) * inv_n
    yn = yc * lax.rsqrt(var + GN_EPS) * lnx_ref[0:1, :] + lnx_ref[1:2, :]
    y_rw = (yn + rwe_ref[0, :, D_RW:2 * D_RW]) * _silu(rwe_ref[0, :, 0:D_RW])
    out = (_bdot(y_rw, wo_ref[0:D_RW, :]) + _bdot(ymla_ref[0], wo_ref[D_RW:D_RW + D_MLA, :])
           + _bdot(ylru_ref[0], wo_ref[D_RW + D_MLA:D_RW + D_MLA + D_LRU, :]))
    z = alpha * x_ref[0] + mod_ref[0, 2:3, :] * out
    mu_z = jnp.mean(z, axis=-1, keepdims=True)
    zc = z - mu_z
    var_z = jnp.mean(zc * zc, axis=-1, keepdims=True)
    o_ref[0] = zc * lax.rsqrt(var_z + LN_EPS) * ln_ref[0:1, :] + ln_ref[1:2, :]


def _post(x, mod, ys, rwe, ymla, ylru, wts, alpha, tile):
    b, l, d = x.shape
    nt = l // tile
    mod_b = mod.shape[0]
    tok = lambda width: pl.BlockSpec((1, tile, width), lambda bi, ti: (bi, ti, 0))
    y_fwd, y_bwd, bwd_group = (ys[0], ys[1], 0) if len(ys) == 2 else (ys[0], ys[0], 1)
    y_spec = lambda grp: pl.BlockSpec((1, tile // SCAN_STEPS, 2, 1, SCAN_STEPS, LANES),
                                      lambda bi, ti: (grp, ti, 0, bi, 0, 0))
    return pl.pallas_call(
        functools.partial(_post_kernel, alpha),
        grid=(b, nt),
        in_specs=[
            tok(d),
            pl.BlockSpec((1, 3, d), (lambda bi, ti: (bi, 0, 0)) if mod_b > 1 else (lambda bi, ti: (0, 0, 0))),
            y_spec(0), y_spec(bwd_group),
            tok(2 * D_RW), tok(D_MLA), tok(D_LRU),
            _const_spec(wts["w_out"].shape), _const_spec(wts["lnx"].shape), _const_spec(wts["ln"].shape),
            _const_spec(wts["ones"].shape),
        ],
        out_specs=tok(d),
        out_shape=jax.ShapeDtypeStruct((b, l, d), jnp.float32),
        compiler_params=_params(("parallel", "parallel")),
        name="post",
    )(x, mod, y_fwd, y_bwd, rwe, ymla, ylru, wts["w_out"], wts["lnx"], wts["ln"], wts["ones"])


def _block_diag(w):
    g, i, o = w.shape
    eye = jnp.eye(g, dtype=w.dtype)
    return jnp.einsum("gio,gh->giho", w, eye).reshape(g * i, g * o)


def _place(cols, offset, width):
    d, n = cols.shape
    return jnp.pad(cols, ((0, 0), (offset, width - offset - n)))


def _pack_layer(l, P, rope):
    bf16 = jnp.bfloat16
    w_in = P["w_in"][l]
    d = w_in.shape[0]
    o_cq = 4 * D_RW
    o_ckv = o_cq + Q_RANK
    o_kr = o_ckv + KV_RANK
    o_gm = o_kr + MLA_ROPE
    o_xl = o_gm + D_MLA
    w_kr = w_in[:, o_kr:o_kr + MLA_ROPE]
    half = MLA_ROPE // 2
    perm = np.concatenate([np.arange(half) * 2, np.arange(half) * 2 + 1])
    perm_sw = np.concatenate([np.arange(half) * 2 + 1, np.arange(half) * 2])
    if rope:
        kra = _place(w_kr[:, perm], MLA_NOPE, SLOT)
        krb = _place(w_kr[:, perm_sw], MLA_NOPE, SLOT)
    else:
        kra = _place(w_kr, 0, SLOT)
        krb = _place(w_kr, MLA_NOPE, SLOT)
    lora = jnp.concatenate([P["rw_w1"][l, 0], P["rw_w1"][l, 1], P["rw_a1"][l, 0], P["rw_a1"][l, 1]], axis=1)
    w_cat = jnp.concatenate([w_in[:, 0:o_kr], kra, krb, w_in[:, o_gm:], lora], axis=1)
    assert w_cat.shape == (d, N_CAT)

    wuq = P["mla_wuq"][l].reshape(Q_RANK, MLA_HEADS, MLA_NOPE + MLA_ROPE)
    qn, qr = wuq[..., :MLA_NOPE], wuq[..., MLA_NOPE:]
    pad_q = lambda rp: jnp.pad(jnp.concatenate([qn, rp], -1), ((0, 0), (0, 0), (0, SLOT - MLA_NOPE - MLA_ROPE)))
    if rope:
        wq_a = pad_q(qr[..., perm])
        wq_b = jnp.pad(qr[..., perm_sw], ((0, 0), (0, 0), (MLA_NOPE, SLOT - MLA_NOPE - MLA_ROPE)))
        wq = jnp.concatenate([wq_a.reshape(Q_RANK, D_SLOTS), wq_b.reshape(Q_RANK, D_SLOTS)], axis=1)
    else:
        wq = pad_q(qr).reshape(Q_RANK, D_SLOTS)
    wukv = P["mla_wukv"][l].reshape(KV_RANK, MLA_HEADS, MLA_NOPE + MLA_V)
    wk = jnp.pad(wukv[..., :MLA_NOPE], ((0, 0), (0, 0), (0, SLOT - MLA_NOPE)))
    wkv = wk.reshape(KV_RANK, D_SLOTS)
    wv_t = wukv[..., MLA_NOPE:].reshape(KV_RANK, D_MLA).T

    zeros_w2 = jnp.zeros_like(P["rw_w2"][l, 0])
    lw2 = jnp.block([[P["rw_w2"][l, 0], zeros_w2], [zeros_w2, P["rw_w2"][l, 1]]])
    la2 = jnp.block([[P["rw_a2"][l, 0], zeros_w2], [zeros_w2, P["rw_a2"][l, 1]]])
    rwp = jnp.concatenate([P["rw_w0"][l], P["rw_a0"][l], P["rw_kk"][l][None], P["rw_ka"][l][None],
                           P["rw_rk"][l].reshape(1, D_RW), jnp.zeros((1, D_RW), jnp.float32)], axis=0)
    grp = jnp.arange(D_RW) // HEAD_RW
    ones = (grp[:, None] == grp[None, :]).astype(bf16)
    wg = jnp.concatenate([_block_diag(P["lru_wa"][l, 0]), _block_diag(P["lru_wa"][l, 1]),
                          _block_diag(P["lru_wx"][l, 0]), _block_diag(P["lru_wx"][l, 1])], axis=1)
    bg = jnp.concatenate([P["lru_ba"][l, 0], P["lru_ba"][l, 1], P["lru_bx"][l, 0], P["lru_bx"][l, 1]])[None]
    return {
        "w_cat": w_cat.astype(bf16), "lw2": lw2.astype(bf16), "la2": la2.astype(bf16), "rwp": rwp,
        "wq": wq.astype(bf16), "wkv": wkv.astype(bf16), "wv_t": wv_t.astype(bf16),
        "qnorm": P["mla_qnorm"][l][None], "kvnorm": P["mla_kvnorm"][l][None], "ones": ones,
        "conv_w": P["lru_conv_w"][l], "conv_b": P["lru_conv_b"][l][None],
        "wg": wg.astype(bf16), "bg": bg, "lam": P["lru_lambda"][l],
        "w_out": P["w_out"][l].astype(bf16),
        "lnx": jnp.stack([P["rw_lnx_g"][l], P["rw_lnx_b"][l]]),
        "ln": jnp.stack([P["ln_g"][l], P["ln_b"][l]]),
        "perm": perm,
    }


def _rope_tables(l):
    rows = l // GRID_W
    row = jnp.repeat(jnp.arange(rows), GRID_W).astype(jnp.float32)
    col = jnp.tile(jnp.arange(GRID_W), rows).astype(jnp.float32)
    pairs = MLA_ROPE // 4
    inv = ROPE_BASE ** (-jnp.arange(pairs, dtype=jnp.float32) / pairs)
    ang = jnp.concatenate([row[:, None] * inv, col[:, None] * inv], axis=-1)
    cos, sin = jnp.cos(ang), jnp.sin(ang)
    tail = jnp.zeros((l, SLOT - MLA_NOPE - MLA_ROPE), jnp.float32)
    nope0 = jnp.zeros((l, MLA_NOPE), jnp.float32)
    ca = jnp.concatenate([nope0, cos, cos, tail], axis=1)
    sb = jnp.concatenate([nope0, -sin, sin, tail], axis=1)
    ta = jnp.concatenate([jnp.ones((l, MLA_NOPE), jnp.float32), cos, cos, tail], axis=1) * Q_SCALE
    tb = sb * Q_SCALE
    return ta, tb, ca, sb


def _layer(x, mod, wts, alpha, ctx, rope_tabs):
    b, l, _ = x.shape
    tile = min(l, 256)
    outs = _pre(x, mod, wts, rope_tabs, tile)
    rwk, rwe, q, k, v, gm, xlgl = outs[:7]
    if ctx is None:
        s0 = jnp.zeros((b, 2, N_HEAD_RW, HEAD_RW, HEAD_RW), jnp.float32)
        h0 = jnp.zeros((b, 2, D_LRU), jnp.float32)
        ctx_kv = None
    else:
        s0, ckv_ctx, kr_ctx, h0 = ctx
        kr_slot = jnp.pad(kr_ctx[..., wts["perm"]], ((0, 0), (0, 0), (MLA_NOPE, SLOT - MLA_NOPE - MLA_ROPE)))
        ctx_kv = _ctx_keyvals(ckv_ctx, kr_slot, wts["wkv"], wts["wv_t"])
    yrw, s_fin = _rwkv(rwk, s0)
    ymla = _attention(q, k, v, gm, ctx_kv, tile)
    ylru, h_fin = _lru(xlgl, h0, wts)
    x_new = _post(x, mod, yrw, rwe, ymla, ylru, wts, alpha, tile)
    extras = (s_fin, outs[7], outs[8], h_fin) if ctx is None else None
    return x_new, extras


def kernel(x_prompt, x_sample, state_rwkv, cache_mla_ckv, cache_mla_krope, state_lru, c, c_ctx, w_mod, b_mod,
           w_in, rw_w0, rw_w1, rw_w2, rw_a0, rw_a1, rw_a2, rw_kk, rw_ka, rw_rk, rw_lnx_g, rw_lnx_b, mla_qnorm,
           mla_wuq, mla_kvnorm, mla_wukv, lru_conv_w, lru_conv_b, lru_wa, lru_ba, lru_wx, lru_bx, lru_lambda,
           w_out, ln_g, ln_b):
    P = dict(w_in=w_in, rw_w0=rw_w0, rw_w1=rw_w1, rw_w2=rw_w2, rw_a0=rw_a0, rw_a1=rw_a1, rw_a2=rw_a2,
             rw_kk=rw_kk, rw_ka=rw_ka, rw_rk=rw_rk, rw_lnx_g=rw_lnx_g, rw_lnx_b=rw_lnx_b, mla_qnorm=mla_qnorm,
             mla_wuq=mla_wuq, mla_kvnorm=mla_kvnorm, mla_wukv=mla_wukv, lru_conv_w=lru_conv_w,
             lru_conv_b=lru_conv_b, lru_wa=lru_wa, lru_ba=lru_ba, lru_wx=lru_wx, lru_bx=lru_bx,
             lru_lambda=lru_lambda, w_out=w_out, ln_g=ln_g, ln_b=ln_b)
    depth = w_in.shape[0]
    alpha = (2 * depth) ** 0.25
    dec_b, dec_l, d = x_sample.shape

    rows = -(-(1 + dec_b) // SUBLANES) * SUBLANES
    cvecs = jnp.concatenate([c_ctx[None, :], c, jnp.zeros((rows - 1 - dec_b, d), jnp.float32)], axis=0)
    mod_all = _modulation(cvecs, w_mod, b_mod).reshape(depth, rows, 3, d)
    rope_tabs = _rope_tables(dec_l)

    xp, xs = x_prompt, x_sample
    rw_states, ckvs, krs, lru_states = [], [], [], []
    for l in range(depth):
        xp, (s_l, ckv_l, kr_l, h_l) = _layer(xp, mod_all[l, 0:1], _pack_layer(l, P, rope=False), alpha, None, None)
        rw_states.append(s_l)
        ckvs.append(ckv_l)
        krs.append(kr_l)
        lru_states.append(h_l)
    for l in range(depth):
        ctx = (state_rwkv[:, l], cache_mla_ckv[:, l], cache_mla_krope[:, l], state_lru[:, l])
        xs, _ = _layer(xs, mod_all[l, 1:1 + dec_b], _pack_layer(l, P, rope=True), alpha, ctx, rope_tabs)
    return (xp, xs, jnp.stack(rw_states, axis=1), jnp.stack(ckvs, axis=1), jnp.stack(krs, axis=1),
            jnp.stack(lru_states, axis=1))
```

```python
import functools
import math

import jax
import jax.numpy as jnp
import numpy as np
from jax import lax
from jax.experimental import pallas as pl
from jax.experimental.pallas import tpu as pltpu

LANES = 128
SUBLANES = 8
VMEM_LIMIT_BYTES = 56 * 1024 * 1024

N_HEAD_RW = 4
HEAD_RW = 64
D_RW = N_HEAD_RW * HEAD_RW
MLA_HEADS = 8
MLA_NOPE = 64
MLA_ROPE = 32
MLA_V = 64
D_MLA = MLA_HEADS * MLA_V
Q_RANK = 256
KV_RANK = 128
D_LRU = 256
LRU_BLOCKS = 4
LRU_BLOCK = 64
GRID_W = 64
ROPE_BASE = 10000.0
SOFTMAX_SCALE = (MLA_NOPE + MLA_ROPE) ** -0.5
Q_SCALE = SOFTMAX_SCALE * math.log2(math.e)
LRU_C = 8.0
GN_EPS = 64e-5
LN_EPS = 1e-5
RMS_EPS = 1e-6
ATTN_TILE = 256
PROJ_TILE = 512
SLOT = LANES
D_SLOTS = MLA_HEADS * SLOT

C_RKVG = 0
C_CQ = C_RKVG + 4 * D_RW
C_CKV = C_CQ + Q_RANK
C_KRA = C_CKV + KV_RANK
C_KRB = C_KRA + SLOT
C_GM = C_KRB + SLOT
C_XL = C_GM + D_MLA
C_GL = C_XL + D_LRU
C_LORA = C_GL + D_LRU
N_CAT = C_LORA + 4 * 64

R_W0, R_A0, R_KK, R_KA, R_RK = 0, 2, 4, 5, 6

SCAN_STEPS = 32
SCAN_UNROLL = 2
SUM_TREE = 8
RWK_R, RWK_NKK, RWK_V, RWK_DIR = 0, 1, 2, 3
N_RWK_CHUNKS = 2 * 9
RWK_SHARED_CHUNKS = 2 * 3


def _params(sem):
    return pltpu.CompilerParams(dimension_semantics=sem, vmem_limit_bytes=VMEM_LIMIT_BYTES)


def _const_spec(shape):
    zeros = (0,) * len(shape)
    return pl.BlockSpec(shape, lambda *_: zeros)


def _split3(x):
    hi = x.astype(jnp.bfloat16)
    r1 = x - hi.astype(jnp.float32)
    mid = r1.astype(jnp.bfloat16)
    lo = (r1 - mid.astype(jnp.float32)).astype(jnp.bfloat16)
    return hi, mid, lo


def _seg_sum(x, ones_bf16):
    hi, mid, lo = _split3(x)
    dot = functools.partial(jnp.dot, preferred_element_type=jnp.float32)
    return dot(hi, ones_bf16) + dot(mid, ones_bf16) + dot(lo, ones_bf16)


def _sigmoid(x):
    return 1.0 / (1.0 + jnp.exp(-x))


def _silu(x):
    return x * _sigmoid(x)


def _softplus(x):
    return jnp.maximum(x, 0.0) + jnp.log1p(jnp.exp(-jnp.abs(x)))


def _bdot(a, b):
    return jnp.dot(a.astype(jnp.bfloat16), b, preferred_element_type=jnp.float32)


_CONTRACT_LAST = (((1,), (1,)), ((), ()))


def _values_t(wv_t, ckvn):
    return lax.dot_general(wv_t, ckvn.astype(jnp.bfloat16), _CONTRACT_LAST,
                           preferred_element_type=jnp.float32).astype(jnp.bfloat16)


def _mod_kernel(c_ref, w_ref, b_ref, o_ref):
    a = _silu(c_ref[...])
    w = w_ref[0]
    a_hi, a_mid, _ = _split3(a)
    w_hi, w_mid, _ = _split3(w)
    dot = functools.partial(jnp.dot, preferred_element_type=jnp.float32)
    o_ref[0] = dot(a_hi, w_hi) + dot(a_hi, w_mid) + dot(a_mid, w_hi) + b_ref[0]


def _modulation(cvecs, w_mod, b_mod):
    depth, d, d3 = w_mod.shape
    rows = cvecs.shape[0]
    nblk = d3 // d
    return pl.pallas_call(
        _mod_kernel,
        grid=(depth, nblk),
        in_specs=[
            pl.BlockSpec((rows, d), lambda l, j: (0, 0)),
            pl.BlockSpec((1, d, d), lambda l, j: (l, 0, j)),
            pl.BlockSpec((1, 1, d), lambda l, j: (l, 0, j)),
        ],
        out_specs=pl.BlockSpec((1, rows, d), lambda l, j: (l, 0, j)),
        out_shape=jax.ShapeDtypeStruct((depth, rows, d3), jnp.float32),
        compiler_params=_params(("parallel", "parallel")),
        name="adaln_mod",
    )(cvecs, w_mod, b_mod.reshape(depth, 1, d3))


def _pre_kernel(rope, *refs):
    if rope:
        (x_ref, mod_ref, win_ref, lw2_ref, la2_ref, rwp_ref, wq_ref, wkv_ref, wvt_ref, qn_ref, kvn_ref,
         ones_ref, ta_ref, tb_ref, ca_ref, sb_ref,
         rwk_ref, rwe_ref, q_ref, k_ref, v_ref, gm_ref, xlgl_ref) = refs
    else:
        (x_ref, mod_ref, win_ref, lw2_ref, la2_ref, rwp_ref, wq_ref, wkv_ref, wvt_ref, qn_ref, kvn_ref,
         ones_ref,
         rwk_ref, rwe_ref, q_ref, k_ref, v_ref, gm_ref, xlgl_ref, ckvn_ref, kr_ref) = refs

    x = x_ref[0]
    shift = mod_ref[0, 0:1, :]
    scale = mod_ref[0, 1:2, :]
    mu = jnp.mean(x, axis=-1, keepdims=True)
    xc = x - mu
    var = jnp.mean(xc * xc, axis=-1, keepdims=True)
    u = xc * lax.rsqrt(var + LN_EPS) * (1.0 + scale) + shift
    h = _bdot(u, win_ref[...])

    r = h[:, 0:D_RW]
    k = h[:, D_RW:2 * D_RW]
    v = h[:, 2 * D_RW:3 * D_RW]
    g = h[:, 3 * D_RW:4 * D_RW]
    lora_w = h[:, C_LORA:C_LORA + 128]
    lora_a = h[:, C_LORA + 128:C_LORA + 256]
    wl = _bdot(jnp.tanh(lora_w), lw2_ref[...])
    al = _bdot(lora_a, la2_ref[...])
    ones = ones_ref[...]
    kk = k * rwp_ref[R_KK:R_KK + 1, :]
    kk = kk / jnp.maximum(jnp.sqrt(_seg_sum(kk * kk, ones)), 1e-12)
    bonus = _seg_sum(r * k * rwp_ref[R_RK:R_RK + 1, :], ones) * v
    ka = rwp_ref[R_KA:R_KA + 1, :]
    def put(slot, val):
        for tb in range(val.shape[0] // SCAN_STEPS):
            for half in range(D_RW // LANES):
                rwk_ref[tb, 2 * slot + half, 0] = val[tb * SCAN_STEPS:(tb + 1) * SCAN_STEPS,
                                                      half * LANES:(half + 1) * LANES]

    put(RWK_R, r)
    put(RWK_NKK, -kk)
    put(RWK_V, v)
    for d in range(2):
        wl_d = wl[:, d * D_RW:(d + 1) * D_RW] + rwp_ref[R_W0 + d:R_W0 + d + 1, :]
        decay = jnp.exp(-math.exp(-0.5) * _sigmoid(wl_d))
        a_d = _sigmoid(al[:, d * D_RW:(d + 1) * D_RW] + rwp_ref[R_A0 + d:R_A0 + d + 1, :])
        put(RWK_DIR + 3 * d, decay)
        put(RWK_DIR + 3 * d + 1, kk * a_d)
        put(RWK_DIR + 3 * d + 2, k * (1.0 + (a_d - 1.0) * ka))
    rwe_ref[0, :, 0:D_RW] = g
    rwe_ref[0, :, D_RW:2 * D_RW] = bonus

    cq = h[:, C_CQ:C_CQ + Q_RANK]
    cqn = cq * lax.rsqrt(jnp.mean(cq * cq, axis=-1, keepdims=True) + RMS_EPS) * qn_ref[...]
    ckv = h[:, C_CKV:C_CKV + KV_RANK]
    ckvn = ckv * lax.rsqrt(jnp.mean(ckv * ckv, axis=-1, keepdims=True) + RMS_EPS) * kvn_ref[...]
    qq = _bdot(cqn, wq_ref[...])
    kv = _bdot(ckvn, wkv_ref[...])
    v_ref[0] = _values_t(wvt_ref[...], ckvn)
    kra = h[:, C_KRA:C_KRA + SLOT]
    krb = h[:, C_KRB:C_KRB + SLOT]
    if rope:
        k_rope = kra * ca_ref[...] + krb * sb_ref[...]
        ta = ta_ref[...]
        tb = tb_ref[...]
    else:
        k_rope = krb
        ckvn_ref[0] = ckvn
        kr_ref[0] = kra[:, 0:MLA_ROPE]
    for hd in range(MLA_HEADS):
        lo, hi = hd * SLOT, (hd + 1) * SLOT
        if rope:
            qh = qq[:, lo:hi] * ta + qq[:, D_SLOTS + lo:D_SLOTS + hi] * tb
        else:
            qh = qq[:, lo:hi] * Q_SCALE
        q_ref[0, :, lo:hi] = qh.astype(jnp.bfloat16)
        k_ref[0, :, lo:hi] = (kv[:, lo:hi] + k_rope).astype(jnp.bfloat16)
    gm_ref[0] = h[:, C_GM:C_GM + D_MLA]
    xlgl_ref[0] = h[:, C_XL:C_XL + 2 * D_LRU]


def _pre(x, mod, wts, rope_tabs, tile):
    b, l, d = x.shape
    rope = rope_tabs is not None
    nt = l // tile
    mod_b = mod.shape[0]
    tok = lambda width: pl.BlockSpec((1, tile, width), lambda bi, ti: (bi, ti, 0))
    in_specs = [
        tok(d),
        pl.BlockSpec((1, 3, d), (lambda bi, ti: (bi, 0, 0)) if mod_b > 1 else (lambda bi, ti: (0, 0, 0))),
        _const_spec(wts["w_cat"].shape), _const_spec(wts["lw2"].shape), _const_spec(wts["la2"].shape),
        _const_spec(wts["rwp"].shape), _const_spec(wts["wq"].shape), _const_spec(wts["wkv"].shape),
        _const_spec(wts["wv_t"].shape), _const_spec(wts["qnorm"].shape), _const_spec(wts["kvnorm"].shape), _const_spec(wts["ones"].shape),
    ]
    args = [x, mod, wts["w_cat"], wts["lw2"], wts["la2"], wts["rwp"], wts["wq"], wts["wkv"], wts["wv_t"],
            wts["qnorm"], wts["kvnorm"], wts["ones"]]
    f32, bf16 = jnp.float32, jnp.bfloat16
    out_shape = [
        jax.ShapeDtypeStruct((l // SCAN_STEPS, N_RWK_CHUNKS, b, SCAN_STEPS, LANES), f32),
        jax.ShapeDtypeStruct((b, l, 2 * D_RW), f32),
        jax.ShapeDtypeStruct((b, l, D_SLOTS), bf16), jax.ShapeDtypeStruct((b, l, D_SLOTS), bf16),
        jax.ShapeDtypeStruct((b, D_MLA, l), bf16), jax.ShapeDtypeStruct((b, l, D_MLA), f32),
        jax.ShapeDtypeStruct((b, l, 2 * D_LRU), f32),
    ]
    rwk_spec = pl.BlockSpec((tile // SCAN_STEPS, N_RWK_CHUNKS, 1, SCAN_STEPS, LANES),
                            lambda bi, ti: (ti, 0, bi, 0, 0))
    vt_spec = pl.BlockSpec((1, D_MLA, tile), lambda bi, ti: (bi, 0, ti))
    out_specs = [rwk_spec, tok(2 * D_RW), tok(D_SLOTS), tok(D_SLOTS), vt_spec, tok(D_MLA), tok(2 * D_LRU)]
    if rope:
        tab = pl.BlockSpec((tile, SLOT), lambda bi, ti: (ti, 0))
        in_specs += [tab, tab, tab, tab]
        args += list(rope_tabs)
    else:
        out_shape += [jax.ShapeDtypeStruct((b, l, KV_RANK), f32), jax.ShapeDtypeStruct((b, l, MLA_ROPE), f32)]
        out_specs += [tok(KV_RANK), tok(MLA_ROPE)]
    return pl.pallas_call(
        functools.partial(_pre_kernel, rope),
        grid=(b, nt), in_specs=in_specs, out_specs=out_specs, out_shape=out_shape,
        compiler_params=_params(("parallel", "parallel")),
        name="pre_rope" if rope else "pre_ctx",
    )(*args)


KT_R, KT_W, KT_B, KT_KD = range(4)
N_KT = 4
PLANES = 2


def _scan_kernel(dl, vs, steps, *refs):
    sh_refs, dr_refs = refs[0:dl], refs[dl:2 * dl]
    s0_ref = refs[2 * dl]
    y_refs = refs[2 * dl + 1:3 * dl + 1]
    sfin_ref = refs[3 * dl + 1]
    s_scr = refs[3 * dl + 2]
    kt_scrs, vv_scrs, nkk_scrs = refs[3 * dl + 3:3 * dl + 5], refs[3 * dl + 5:3 * dl + 7], refs[3 * dl + 7:]
    g = pl.program_id(0)
    i = pl.program_id(1)
    rows_per_chunk = sh_refs[0].shape[1] // RWK_SHARED_CHUNKS
    batch = rows_per_chunk // steps
    vr = HEAD_RW // vs
    lane_split = (lax.broadcasted_iota(jnp.int32, (vr, LANES), 1) // (2 * batch)) % vs

    def block_row(e, j):
        if dl == 2:
            return j if e == 0 else steps - 1 - j
        return j + g * (steps - 1 - 2 * j)

    def batch_rows(ref, chunk, t):
        return ref[0, pl.ds(chunk * rows_per_chunk + t, batch, stride=steps), :]

    def gather(in_refs, slot, j):
        parts = []
        for e in range(dl):
            t = block_row(e, j)
            pair = [batch_rows(in_refs[e], 2 * slot + hp, t) for hp in range(2)]
            parts += pair * vs
        return jnp.concatenate(parts, axis=0)

    @pl.when(i == 0)
    def _():
        s_scr[...] = s0_ref[0]

    def prepare_tile(kt, tile, in_refs, slot, j):
        kt[tile] = gather(in_refs, slot, j).T

    def prepare_v(j, buf):
        v_t = gather(sh_refs, RWK_V, j).T
        for p in range(PLANES):
            base = p * HEAD_RW
            vv = v_t[base:base + vr, :]
            for s in range(1, vs):
                vv = jnp.where(lane_split == s, v_t[base + s * vr:base + (s + 1) * vr, :], vv)
            vv_scrs[buf][p] = vv

    def prepare_nkk(j, buf):
        nkk_scrs[buf][...] = gather(sh_refs, RWK_NKK, jnp.minimum(j, steps - 1)).T

    def row(ref, *idx):
        return jnp.broadcast_to(ref[idx[:-1] + (pl.ds(idx[-1], 1), slice(None))], (vr, LANES))

    def tree_sum(terms):
        while len(terms) > 1:
            terms = [terms[a] + terms[a + 1] for a in range(0, len(terms), 2)]
        return terms[0]

    def emit(j, y_planes):
        zero = jnp.zeros((vr, LANES), jnp.float32)
        y_full = jnp.concatenate([jnp.where(lane_split == s, y_planes[p], zero)
                                  for p in range(PLANES) for s in range(vs)], axis=0)
        y_t = y_full.T
        for e in range(dl):
            t = block_row(e, j)
            for hp in range(2):
                parts = [y_t[((e * vs + s) * 2 + hp) * batch:((e * vs + s) * 2 + hp + 1) * batch, :]
                         for s in range(vs)]
                y_refs[e][0, 0, pl.ds(hp * rows_per_chunk + t, batch, stride=steps), :] = tree_sum(parts)

    for tile, in_refs, slot in ((KT_R, sh_refs, RWK_R), (KT_W, dr_refs, 0), (KT_B, dr_refs, 1), (KT_KD, dr_refs, 2)):
        prepare_tile(kt_scrs[0], tile, in_refs, slot, 0)
    prepare_v(0, 0)
    prepare_nkk(0, 0)
    prepare_nkk(1, 1)
    sa_first = tuple(
        tree_sum([tree_sum([s_scr[p, k] * row(nkk_scrs[0], p * HEAD_RW + k) for k in range(k0, HEAD_RW, 4)])
                  for k0 in range(4)])
        for p in range(PLANES))

    def step(j, cur, sa, y_prev):
        nxt = 1 - cur
        kt, vv_ref, nkk_next = kt_scrs[cur], vv_scrs[cur], nkk_scrs[nxt]
        j1 = jnp.minimum(j + 1, steps - 1)
        side_work = [
            lambda: emit(jnp.maximum(j - 1, 0), y_prev),
            lambda: prepare_tile(kt_scrs[nxt], KT_W, dr_refs, 0, j1),
            lambda: prepare_tile(kt_scrs[nxt], KT_B, dr_refs, 1, j1),
            lambda: prepare_tile(kt_scrs[nxt], KT_KD, dr_refs, 2, j1),
            lambda: prepare_tile(kt_scrs[nxt], KT_R, sh_refs, RWK_R, j1),
            lambda: prepare_v(j1, nxt),
            lambda: prepare_nkk(j + 2, cur),
        ]
        chunk = PLANES * HEAD_RW // (len(side_work) + 1)
        y_planes, sa_next = [], []
        done = 0
        for p in range(PLANES):
            base = p * HEAD_RW
            vv = vv_ref[p]
            y_acc, sa_acc, y_terms, sa_terms = None, None, [], []
            for k in range(HEAD_RW):
                s_new = s_scr[p, k] * row(kt, KT_W, base + k) + (
                    sa[p] * row(kt, KT_B, base + k) + vv * row(kt, KT_KD, base + k))
                s_scr[p, k] = s_new
                y_terms.append(s_new * row(kt, KT_R, base + k))
                sa_terms.append(s_new * row(nkk_next, base + k))
                if len(y_terms) == SUM_TREE:
                    y_sum, sa_sum = tree_sum(y_terms), tree_sum(sa_terms)
                    y_acc = y_sum if y_acc is None else y_acc + y_sum
                    sa_acc = sa_sum if sa_acc is None else sa_acc + sa_sum
                    y_terms, sa_terms = [], []
                done += 1
                if done % chunk == 0 and side_work:
                    side_work.pop(0)()
            y_planes.append(y_acc)
            sa_next.append(sa_acc)
        return tuple(sa_next), tuple(y_planes)

    def unrolled_steps(jj, carry):
        for u in range(SCAN_UNROLL):
            carry = step(SCAN_UNROLL * jj + u, u % 2, *carry)
        return carry

    zeros = tuple(jnp.zeros((vr, LANES), jnp.float32) for _ in range(PLANES))
    _, y_last = lax.fori_loop(0, steps // SCAN_UNROLL, unrolled_steps, (sa_first, zeros))
    emit(steps - 1, y_last)

    @pl.when(i == pl.num_programs(1) - 1)
    def _():
        sfin_ref[0] = s_scr[...]


def _lane_plan(batch):
    per_dir = (N_HEAD_RW // PLANES) * batch
    assert LANES % per_dir == 0
    fill = LANES // per_dir
    dl = 2 if fill >= 4 else 1
    vs = fill // dl
    assert dl * vs * per_dir == LANES and HEAD_RW % (vs * SUBLANES) == 0
    return 2 // dl, dl, vs


def _rwkv_scan(rwk, s0_lanes, plan):
    groups, dl, vs = plan
    nt, chunks, b, steps, _ = rwk.shape
    vr = HEAD_RW // vs
    rows = b * steps
    rwk_rows = rwk.reshape(nt, chunks * rows, LANES)

    def time_block(d, ti):
        return ti + d * (nt - 1 - 2 * ti)

    def slot_dir(e, gi):
        return e if dl == 2 else gi

    blk = (1, RWK_SHARED_CHUNKS * rows, LANES)
    in_specs = [pl.BlockSpec(blk, functools.partial(lambda e, gi, ti: (time_block(slot_dir(e, gi), ti), 0, 0), e))
                for e in range(dl)]
    in_specs += [pl.BlockSpec(blk, functools.partial(
        lambda e, gi, ti: (time_block(slot_dir(e, gi), ti), 1 + slot_dir(e, gi), 0), e)) for e in range(dl)]
    state_spec = pl.BlockSpec((1, PLANES, HEAD_RW, vr, LANES), lambda gi, ti: (gi, 0, 0, 0, 0))
    in_specs.append(state_spec)
    y_specs = [pl.BlockSpec((1, 1, 2 * rows, LANES), functools.partial(
        lambda e, gi, ti: (gi, time_block(slot_dir(e, gi), ti), 0, 0), e)) for e in range(dl)]
    outs = pl.pallas_call(
        functools.partial(_scan_kernel, dl, vs, steps),
        grid=(groups, nt),
        in_specs=in_specs,
        out_specs=y_specs + [state_spec],
        out_shape=[jax.ShapeDtypeStruct((groups, nt, 2 * rows, LANES), jnp.float32)] * dl
        + [jax.ShapeDtypeStruct((groups, PLANES, HEAD_RW, vr, LANES), jnp.float32)],
        scratch_shapes=[pltpu.VMEM((PLANES, HEAD_RW, vr, LANES), jnp.float32),
                        pltpu.VMEM((N_KT, LANES, LANES), jnp.float32), pltpu.VMEM((N_KT, LANES, LANES), jnp.float32),
                        pltpu.VMEM((PLANES, vr, LANES), jnp.float32), pltpu.VMEM((PLANES, vr, LANES), jnp.float32),
                        pltpu.VMEM((LANES, LANES), jnp.float32), pltpu.VMEM((LANES, LANES), jnp.float32)],
        compiler_params=_params(("parallel", "arbitrary")),
        name="rwkv_scan",
    )(*([rwk_rows] * (2 * dl)), s0_lanes)
    ys = [y.reshape(groups, nt, 2, b, steps, LANES) for y in outs[:dl]]
    return ys, outs[dl]


def _state_to_lanes(s, plan):
    g, dl, vs = plan
    b = s.shape[0]
    vr = HEAD_RW // vs
    t = s.reshape(b, g, dl, N_HEAD_RW // PLANES, PLANES, vs, vr, HEAD_RW)
    t = jnp.transpose(t, (1, 4, 7, 6, 2, 5, 3, 0))
    return t.reshape(g, PLANES, HEAD_RW, vr, LANES)


def _state_from_lanes(s, plan, b):
    g, dl, vs = plan
    vr = HEAD_RW // vs
    t = s.reshape(g, PLANES, HEAD_RW, vr, dl, vs, N_HEAD_RW // PLANES, b)
    t = jnp.transpose(t, (7, 0, 4, 6, 1, 5, 3, 2))
    return t.reshape(b, 2, N_HEAD_RW, HEAD_RW, HEAD_RW)


def _rwkv(rwk, s0):
    b = rwk.shape[2]
    plan = _lane_plan(b)
    ys, s_fin = _rwkv_scan(rwk, _state_to_lanes(s0, plan), plan)
    return ys, _state_from_lanes(s_fin, plan, b)


ROW_GROUP = 64


def _reduce_rows(x, op):
    r, c = x.shape
    if r > ROW_GROUP and r % ROW_GROUP == 0:
        x = op(x.reshape(r // ROW_GROUP, ROW_GROUP, c), axis=0)
    return op(x, axis=0, keepdims=True)


def _attn_kernel(has_ctx, *refs):
    if has_ctx:
        q_ref, k_ref, v_ref, kc_ref, vc_ref, gm_ref, o_ref = refs
    else:
        q_ref, k_ref, v_ref, gm_ref, o_ref = refs
    def scores(hd):
        sl = slice(hd * SLOT, (hd + 1) * SLOT)
        q = q_ref[0, :, sl]
        s = lax.dot_general(k_ref[0, :, sl], q, _CONTRACT_LAST, preferred_element_type=jnp.float32)
        sc = (lax.dot_general(kc_ref[0, :, sl], q, _CONTRACT_LAST, preferred_element_type=jnp.float32)
              if has_ctx else None)
        return s, sc

    nxt = scores(0)
    outs = []
    for hd in range(MLA_HEADS):
        s, sc = nxt
        if hd + 1 < MLA_HEADS:
            nxt = scores(hd + 1)
        vrows = slice(hd * MLA_V, (hd + 1) * MLA_V)
        m = _reduce_rows(s, jnp.max)
        if has_ctx:
            m = jnp.maximum(m, _reduce_rows(sc, jnp.max))
        p = jnp.exp2(s - m)
        den = _reduce_rows(p, jnp.sum)
        o = jnp.dot(v_ref[0, vrows, :], p.astype(jnp.bfloat16), preferred_element_type=jnp.float32)
        if has_ctx:
            pc = jnp.exp2(sc - m)
            den = den + _reduce_rows(pc, jnp.sum)
            o = o + jnp.dot(vc_ref[0, vrows, :], pc.astype(jnp.bfloat16), preferred_element_type=jnp.float32)
        outs.append(o * (1.0 / den))
        if hd % 2 == 1:
            cols = slice((hd // 2) * LANES, (hd // 2 + 1) * LANES)
            o_ref[0, :, cols] = jnp.concatenate(outs, axis=0).T * _silu(gm_ref[0, :, cols])
            outs = []


def _attention(q, k, v, gm, ctx_kv, tile):
    b, l, _ = q.shape
    nt = l // tile
    has_ctx = ctx_kv is not None
    tok = lambda width: pl.BlockSpec((1, tile, width), lambda bi, ti: (bi, ti, 0))
    full = lambda arr: pl.BlockSpec((1,) + arr.shape[1:], lambda bi, ti: (bi, 0, 0))
    in_specs = [tok(D_SLOTS), full(k), full(v)]
    args = [q, k, v]
    if has_ctx:
        in_specs += [full(ctx_kv[0]), full(ctx_kv[1])]
        args += list(ctx_kv)
    in_specs.append(tok(D_MLA))
    args.append(gm)
    return pl.pallas_call(
        functools.partial(_attn_kernel, has_ctx),
        grid=(b, nt), in_specs=in_specs, out_specs=tok(D_MLA),
        out_shape=jax.ShapeDtypeStruct((b, l, D_MLA), jnp.float32),
        compiler_params=_params(("parallel", "parallel")),
        name="mla_attn_ctx" if has_ctx else "mla_attn",
    )(*args)


def _ctxkv_kernel(ckv_ref, kr_ref, wkv_ref, wvt_ref, k_ref, v_ref):
    kv = _bdot(ckv_ref[0], wkv_ref[...])
    kr = kr_ref[0]
    for hd in range(MLA_HEADS):
        sl = slice(hd * SLOT, (hd + 1) * SLOT)
        k_ref[0, :, sl] = (kv[:, sl] + kr).astype(jnp.bfloat16)
    v_ref[0] = _values_t(wvt_ref[...], ckv_ref[0])


def _ctx_keyvals(ckv_ctx, kr_slot, wkv, wv_t):
    b, p, _ = ckv_ctx.shape
    blk = lambda width: pl.BlockSpec((1, p, width), lambda bi: (bi, 0, 0))
    return pl.pallas_call(
        _ctxkv_kernel, grid=(b,),
        in_specs=[blk(KV_RANK), blk(SLOT), _const_spec(wkv.shape), _const_spec(wv_t.shape)],
        out_specs=[blk(D_SLOTS), pl.BlockSpec((1, D_MLA, p), lambda bi: (bi, 0, 0))],
        out_shape=[jax.ShapeDtypeStruct((b, p, D_SLOTS), jnp.bfloat16),
                   jax.ShapeDtypeStruct((b, D_MLA, p), jnp.bfloat16)],
        compiler_params=_params(("parallel",)),
        name="ctx_keyvals",
    )(ckv_ctx, kr_slot, wkv, wv_t)


def _lru_kernel(xlgl_ref, cw_ref, cb_ref, wg_ref, bg_ref, lam_ref, h0_ref, y_ref, hfin_ref,
                a_scr, b_scr, h_scr):
    l = xlgl_ref.shape[1]
    xl = xlgl_ref[0, :, 0:D_LRU]
    t_idx = lax.broadcasted_iota(jnp.int32, (l, D_LRU), 0)
    xc = cb_ref[...] + cw_ref[2:3, :] * xl
    for j, back in ((0, 2), (1, 1), (3, -1)):
        rolled = pltpu.roll(xl, back % l, 0)
        ok = (t_idx >= back) if back > 0 else (t_idx < l + back)
        xc = xc + cw_ref[j:j + 1, :] * jnp.where(ok, rolled, 0.0)
    xcb = xc.astype(jnp.bfloat16)
    r_idx = lax.broadcasted_iota(jnp.int32, (SUBLANES, D_LRU), 0)
    nblk = l // SUBLANES
    for d in range(2):
        gr = jnp.dot(xcb, wg_ref[:, d * D_LRU:(d + 1) * D_LRU], preferred_element_type=jnp.float32)
        gr = gr + bg_ref[:, d * D_LRU:(d + 1) * D_LRU]
        gi = jnp.dot(xcb, wg_ref[:, (2 + d) * D_LRU:(3 + d) * D_LRU], preferred_element_type=jnp.float32)
        gi = gi + bg_ref[:, (2 + d) * D_LRU:(3 + d) * D_LRU]
        log_a = -LRU_C * _sigmoid(gr) * _softplus(-lam_ref[d:d + 1, :])
        a = jnp.exp(log_a)
        a_scr[d] = a
        b_scr[d] = jnp.sqrt(1.0 - a * a) * _sigmoid(gi) * xc

    def block(i, carries):
        new = []
        for d in range(2):
            blk = i if d == 0 else nblk - 1 - i
            rs = pl.ds(pl.multiple_of(blk * SUBLANES, SUBLANES), SUBLANES)
            aa = a_scr[d, rs, :]
            bb = b_scr[d, rs, :]
            for s in (1, 2, 4):
                if d == 0:
                    keep = r_idx >= s
                    sh = s
                else:
                    keep = r_idx < SUBLANES - s
                    sh = SUBLANES - s
                a_sh = jnp.where(keep, pltpu.roll(aa, sh, 0), 1.0)
                b_sh = jnp.where(keep, pltpu.roll(bb, sh, 0), 0.0)
                bb = bb + aa * b_sh
                aa = aa * a_sh
            hh = bb + aa * carries[d]
            h_scr[d, rs, :] = hh
            edge = SUBLANES - 1 if d == 0 else 0
            new.append(jnp.broadcast_to(hh[edge:edge + 1, :], (SUBLANES, D_LRU)))
        return tuple(new)

    last = lax.fori_loop(0, nblk, block,
                         tuple(jnp.broadcast_to(h0_ref[0, d:d + 1, :], (SUBLANES, D_LRU)) for d in range(2)))
    for d in range(2):
        hfin_ref[0, d:d + 1, :] = last[d][0:1, :]
    gl = xlgl_ref[0, :, D_LRU:2 * D_LRU]
    y_ref[0] = (h_scr[0] + h_scr[1]) * _silu(gl)


def _lru(xlgl, h0, wts):
    b, l, _ = xlgl.shape
    return pl.pallas_call(
        _lru_kernel, grid=(b,),
        in_specs=[
            pl.BlockSpec((1, l, 2 * D_LRU), lambda bi: (bi, 0, 0)),
            _const_spec(wts["conv_w"].shape), _const_spec(wts["conv_b"].shape),
            _const_spec(wts["wg"].shape), _const_spec(wts["bg"].shape), _const_spec(wts["lam"].shape),
            pl.BlockSpec((1, 2, D_LRU), lambda bi: (bi, 0, 0)),
        ],
        out_specs=[pl.BlockSpec((1, l, D_LRU), lambda bi: (bi, 0, 0)),
                   pl.BlockSpec((1, 2, D_LRU), lambda bi: (bi, 0, 0))],
        out_shape=[jax.ShapeDtypeStruct((b, l, D_LRU), jnp.float32),
                   jax.ShapeDtypeStruct((b, 2, D_LRU), jnp.float32)],
        scratch_shapes=[pltpu.VMEM((2, l, D_LRU), jnp.float32), pltpu.VMEM((2, l, D_LRU), jnp.float32),
                        pltpu.VMEM((2, l, D_LRU), jnp.float32)],
        compiler_params=_params(("parallel",)),
        name="rglru",
    )(xlgl, wts["conv_w"], wts["conv_b"], wts["wg"], wts["bg"], wts["lam"], h0)


def _post_kernel(alpha, x_ref, mod_ref, yf_ref, yb_ref, rwe_ref, ymla_ref, ylru_ref, wo_ref, lnx_ref, ln_ref,
                 ones_ref, o_ref):
    y = jnp.concatenate(
        [jnp.concatenate([yf_ref[0, tb, hp, 0] + yb_ref[0, tb, hp, 0] for tb in range(yf_ref.shape[1])], axis=0)
         for hp in range(D_RW // LANES)], axis=-1)
    ones = ones_ref[...]
    inv_n = 1.0 / HEAD_RW
    mu = _seg_sum(y, ones) * inv_n
    yc = y - mu
    var = _seg_sum(yc * yc, ones) * inv_n
    yn = yc * lax.rsqrt(var + GN_EPS) * lnx_ref[0:1, :] + lnx_ref[1:2, :]
    y_rw = (yn + rwe_ref[0, :, D_RW:2 * D_RW]) * _silu(rwe_ref[0, :, 0:D_RW])
    out = (_bdot(y_rw, wo_ref[0:D_RW, :]) + _bdot(ymla_ref[0], wo_ref[D_RW:D_RW + D_MLA, :])
           + _bdot(ylru_ref[0], wo_ref[D_RW + D_MLA:D_RW + D_MLA + D_LRU, :]))
    z = alpha * x_ref[0] + mod_ref[0, 2:3, :] * out
    mu_z = jnp.mean(z, axis=-1, keepdims=True)
    zc = z - mu_z
    var_z = jnp.mean(zc * zc, axis=-1, keepdims=True)
    o_ref[0] = zc * lax.rsqrt(var_z + LN_EPS) * ln_ref[0:1, :] + ln_ref[1:2, :]


def _post(x, mod, ys, rwe, ymla, ylru, wts, alpha, tile):
    b, l, d = x.shape
    nt = l // tile
    mod_b = mod.shape[0]
    tok = lambda width: pl.BlockSpec((1, tile, width), lambda bi, ti: (bi, ti, 0))
    y_fwd, y_bwd, bwd_group = (ys[0], ys[1], 0) if len(ys) == 2 else (ys[0], ys[0], 1)
    y_spec = lambda grp: pl.BlockSpec((1, tile // SCAN_STEPS, 2, 1, SCAN_STEPS, LANES),
                                      lambda bi, ti: (grp, ti, 0, bi, 0, 0))
    return pl.pallas_call(
        functools.partial(_post_kernel, alpha),
        grid=(b, nt),
        in_specs=[
            tok(d),
            pl.BlockSpec((1, 3, d), (lambda bi, ti: (bi, 0, 0)) if mod_b > 1 else (lambda bi, ti: (0, 0, 0))),
            y_spec(0), y_spec(bwd_group),
            tok(2 * D_RW), tok(D_MLA), tok(D_LRU),
            _const_spec(wts["w_out"].shape), _const_spec(wts["lnx"].shape), _const_spec(wts["ln"].shape),
            _const_spec(wts["ones"].shape),
        ],
        out_specs=tok(d),
        out_shape=jax.ShapeDtypeStruct((b, l, d), jnp.float32),
        compiler_params=_params(("parallel", "parallel")),
        name="post",
    )(x, mod, y_fwd, y_bwd, rwe, ymla, ylru, wts["w_out"], wts["lnx"], wts["ln"], wts["ones"])


def _block_diag(w):
    g, i, o = w.shape
    eye = jnp.eye(g, dtype=w.dtype)
    return jnp.einsum("gio,gh->giho", w, eye).reshape(g * i, g * o)


def _place(cols, offset, width):
    d, n = cols.shape
    return jnp.pad(cols, ((0, 0), (offset, width - offset - n)))


def _pack_layer(l, P, rope):
    bf16 = jnp.bfloat16
    w_in = P["w_in"][l]
    d = w_in.shape[0]
    o_cq = 4 * D_RW
    o_ckv = o_cq + Q_RANK
    o_kr = o_ckv + KV_RANK
    o_gm = o_kr + MLA_ROPE
    o_xl = o_gm + D_MLA
    w_kr = w_in[:, o_kr:o_kr + MLA_ROPE]
    half = MLA_ROPE // 2
    perm = np.concatenate([np.arange(half) * 2, np.arange(half) * 2 + 1])
    perm_sw = np.concatenate([np.arange(half) * 2 + 1, np.arange(half) * 2])
    if rope:
        kra = _place(w_kr[:, perm], MLA_NOPE, SLOT)
        krb = _place(w_kr[:, perm_sw], MLA_NOPE, SLOT)
    else:
        kra = _place(w_kr, 0, SLOT)
        krb = _place(w_kr, MLA_NOPE, SLOT)
    lora = jnp.concatenate([P["rw_w1"][l, 0], P["rw_w1"][l, 1], P["rw_a1"][l, 0], P["rw_a1"][l, 1]], axis=1)
    w_cat = jnp.concatenate([w_in[:, 0:o_kr], kra, krb, w_in[:, o_gm:], lora], axis=1)
    assert w_cat.shape == (d, N_CAT)

    wuq = P["mla_wuq"][l].reshape(Q_RANK, MLA_HEADS, MLA_NOPE + MLA_ROPE)
    qn, qr = wuq[..., :MLA_NOPE], wuq[..., MLA_NOPE:]
    pad_q = lambda rp: jnp.pad(jnp.concatenate([qn, rp], -1), ((0, 0), (0, 0), (0, SLOT - MLA_NOPE - MLA_ROPE)))
    if rope:
        wq_a = pad_q(qr[..., perm])
        wq_b = jnp.pad(qr[..., perm_sw], ((0, 0), (0, 0), (MLA_NOPE, SLOT - MLA_NOPE - MLA_ROPE)))
        wq = jnp.concatenate([wq_a.reshape(Q_RANK, D_SLOTS), wq_b.reshape(Q_RANK, D_SLOTS)], axis=1)
    else:
        wq = pad_q(qr).reshape(Q_RANK, D_SLOTS)
    wukv = P["mla_wukv"][l].reshape(KV_RANK, MLA_HEADS, MLA_NOPE + MLA_V)
    wk = jnp.pad(wukv[..., :MLA_NOPE], ((0, 0), (0, 0), (0, SLOT - MLA_NOPE)))
    wkv = wk.reshape(KV_RANK, D_SLOTS)
    wv_t = wukv[..., MLA_NOPE:].reshape(KV_RANK, D_MLA).T

    zeros_w2 = jnp.zeros_like(P["rw_w2"][l, 0])
    lw2 = jnp.block([[P["rw_w2"][l, 0], zeros_w2], [zeros_w2, P["rw_w2"][l, 1]]])
    la2 = jnp.block([[P["rw_a2"][l, 0], zeros_w2], [zeros_w2, P["rw_a2"][l, 1]]])
    rwp = jnp.concatenate([P["rw_w0"][l], P["rw_a0"][l], P["rw_kk"][l][None], P["rw_ka"][l][None],
                           P["rw_rk"][l].reshape(1, D_RW), jnp.zeros((1, D_RW), jnp.float32)], axis=0)
    grp = jnp.arange(D_RW) // HEAD_RW
    ones = (grp[:, None] == grp[None, :]).astype(bf16)
    wg = jnp.concatenate([_block_diag(P["lru_wa"][l, 0]), _block_diag(P["lru_wa"][l, 1]),
                          _block_diag(P["lru_wx"][l, 0]), _block_diag(P["lru_wx"][l, 1])], axis=1)
    bg = jnp.concatenate([P["lru_ba"][l, 0], P["lru_ba"][l, 1], P["lru_bx"][l, 0], P["lru_bx"][l, 1]])[None]
    return {
        "w_cat": w_cat.astype(bf16), "lw2": lw2.astype(bf16), "la2": la2.astype(bf16), "rwp": rwp,
        "wq": wq.astype(bf16), "wkv": wkv.astype(bf16), "wv_t": wv_t.astype(bf16),
        "qnorm": P["mla_qnorm"][l][None], "kvnorm": P["mla_kvnorm"][l][None], "ones": ones,
        "conv_w": P["lru_conv_w"][l], "conv_b": P["lru_conv_b"][l][None],
        "wg": wg.astype(bf16), "bg": bg, "lam": P["lru_lambda"][l],
        "w_out": P["w_out"][l].astype(bf16),
        "lnx": jnp.stack([P["rw_lnx_g"][l], P["rw_lnx_b"][l]]),
        "ln": jnp.stack([P["ln_g"][l], P["ln_b"][l]]),
        "perm": perm,
    }


def _rope_tables(l):
    rows = l // GRID_W
    row = jnp.repeat(jnp.arange(rows), GRID_W).astype(jnp.float32)
    col = jnp.tile(jnp.arange(GRID_W), rows).astype(jnp.float32)
    pairs = MLA_ROPE // 4
    inv = ROPE_BASE ** (-jnp.arange(pairs, dtype=jnp.float32) / pairs)
    ang = jnp.concatenate([row[:, None] * inv, col[:, None] * inv], axis=-1)
    cos, sin = jnp.cos(ang), jnp.sin(ang)
    tail = jnp.zeros((l, SLOT - MLA_NOPE - MLA_ROPE), jnp.float32)
    nope0 = jnp.zeros((l, MLA_NOPE), jnp.float32)
    ca = jnp.concatenate([nope0, cos, cos, tail], axis=1)
    sb = jnp.concatenate([nope0, -sin, sin, tail], axis=1)
    ta = jnp.concatenate([jnp.ones((l, MLA_NOPE), jnp.float32), cos, cos, tail], axis=1) * Q_SCALE
    tb = sb * Q_SCALE
    return ta, tb, ca, sb


def _layer(x, mod, wts, alpha, ctx, rope_tabs):
    b, l, _ = x.shape
    tile = min(l, ATTN_TILE)
    proj_tile = min(l, PROJ_TILE)
    outs = _pre(x, mod, wts, rope_tabs, proj_tile)
    rwk, rwe, q, k, v, gm, xlgl = outs[:7]
    if ctx is None:
        s0 = jnp.zeros((b, 2, N_HEAD_RW, HEAD_RW, HEAD_RW), jnp.float32)
        h0 = jnp.zeros((b, 2, D_LRU), jnp.float32)
        ctx_kv = None
    else:
        s0, ckv_ctx, kr_ctx, h0 = ctx
        kr_slot = jnp.pad(kr_ctx[..., wts["perm"]], ((0, 0), (0, 0), (MLA_NOPE, SLOT - MLA_NOPE - MLA_ROPE)))
        ctx_kv = _ctx_keyvals(ckv_ctx, kr_slot, wts["wkv"], wts["wv_t"])
    yrw, s_fin = _rwkv(rwk, s0)
    ymla = _attention(q, k, v, gm, ctx_kv, tile)
    ylru, h_fin = _lru(xlgl, h0, wts)
    x_new = _post(x, mod, yrw, rwe, ymla, ylru, wts, alpha, proj_tile)
    extras = (s_fin, outs[7], outs[8], h_fin) if ctx is None else None
    return x_new, extras


def kernel(x_prompt, x_sample, state_rwkv, cache_mla_ckv, cache_mla_krope, state_lru, c, c_ctx, w_mod, b_mod,
           w_in, rw_w0, rw_w1, rw_w2, rw_a0, rw_a1, rw_a2, rw_kk, rw_ka, rw_rk, rw_lnx_g, rw_lnx_b, mla_qnorm,
           mla_wuq, mla_kvnorm, mla_wukv, lru_conv_w, lru_conv_b, lru_wa, lru_ba, lru_wx, lru_bx, lru_lambda,
           w_out, ln_g, ln_b):
    P = dict(w_in=w_in, rw_w0=rw_w0, rw_w1=rw_w1, rw_w2=rw_w2, rw_a0=rw_a0, rw_a1=rw_a1, rw_a2=rw_a2,
             rw_kk=rw_kk, rw_ka=rw_ka, rw_rk=rw_rk, rw_lnx_g=rw_lnx_g, rw_lnx_b=rw_lnx_b, mla_qnorm=mla_qnorm,
             mla_wuq=mla_wuq, mla_kvnorm=mla_kvnorm, mla_wukv=mla_wukv, lru_conv_w=lru_conv_w,
             lru_conv_b=lru_conv_b, lru_wa=lru_wa, lru_ba=lru_ba, lru_wx=lru_wx, lru_bx=lru_bx,
             lru_lambda=lru_lambda, w_out=w_out, ln_g=ln_g, ln_b=ln_b)
    depth = w_in.shape[0]
    alpha = (2 * depth) ** 0.25
    dec_b, dec_l, d = x_sample.shape

    rows = -(-(1 + dec_b) // SUBLANES) * SUBLANES
    cvecs = jnp.concatenate([c_ctx[None, :], c, jnp.zeros((rows - 1 - dec_b, d), jnp.float32)], axis=0)
    mod_all = _modulation(cvecs, w_mod, b_mod).reshape(depth, rows, 3, d)
    rope_tabs = _rope_tables(dec_l)

    xp, xs = x_prompt, x_sample
    rw_states, ckvs, krs, lru_states = [], [], [], []
    for l in range(depth):
        xp, (s_l, ckv_l, kr_l, h_l) = _layer(xp, mod_all[l, 0:1], _pack_layer(l, P, rope=False), alpha, None, None)
        rw_states.append(s_l)
        ckvs.append(ckv_l)
        krs.append(kr_l)
        lru_states.append(h_l)
    for l in range(depth):
        ctx = (state_rwkv[:, l], cache_mla_ckv[:, l], cache_mla_krope[:, l], state_lru[:, l])
        xs, _ = _layer(xs, mod_all[l, 1:1 + dec_b], _pack_layer(l, P, rope=True), alpha, ctx, rope_tabs)
    return (xp, xs, jnp.stack(rw_states, axis=1), jnp.stack(ckvs, axis=1), jnp.stack(krs, axis=1),
            jnp.stack(lru_states, axis=1))
```

```python
import functools
import math

import jax
import jax.numpy as jnp
import numpy as np
from jax import lax
from jax.experimental import pallas as pl
from jax.experimental.pallas import tpu as pltpu

LANES = 128
SUBLANES = 8
VMEM_LIMIT_BYTES = 56 * 1024 * 1024

N_HEAD_RW = 4
HEAD_RW = 64
D_RW = N_HEAD_RW * HEAD_RW
MLA_HEADS = 8
MLA_NOPE = 64
MLA_ROPE = 32
MLA_V = 64
D_MLA = MLA_HEADS * MLA_V
Q_RANK = 256
KV_RANK = 128
D_LRU = 256
LRU_BLOCKS = 4
LRU_BLOCK = 64
GRID_W = 64
ROPE_BASE = 10000.0
SOFTMAX_SCALE = (MLA_NOPE + MLA_ROPE) ** -0.5
Q_SCALE = SOFTMAX_SCALE * math.log2(math.e)
LRU_C = 8.0
GN_EPS = 64e-5
LN_EPS = 1e-5
RMS_EPS = 1e-6
ATTN_TILE = 256
PROJ_TILE = 512
SLOT = LANES
D_SLOTS = MLA_HEADS * SLOT

C_RKVG = 0
C_CQ = C_RKVG + 4 * D_RW
C_CKV = C_CQ + Q_RANK
C_KRA = C_CKV + KV_RANK
C_KRB = C_KRA + SLOT
C_GM = C_KRB + SLOT
C_XL = C_GM + D_MLA
C_GL = C_XL + D_LRU
C_LORA = C_GL + D_LRU
N_CAT = C_LORA + 4 * 64

R_W0, R_A0, R_KK, R_KA, R_RK = 0, 2, 4, 5, 6

SCAN_STEPS = 16
SCAN_BLOCK_BYTES = 13 * 256 * 1024
SCAN_UNROLL = 2
SUM_TREE = 8
RWK_R, RWK_NKK, RWK_V, RWK_DIR = 0, 1, 2, 3
N_RWK_CHUNKS = 2 * 9
RWK_SHARED_CHUNKS = 2 * 3


def _params(sem):
    return pltpu.CompilerParams(dimension_semantics=sem, vmem_limit_bytes=VMEM_LIMIT_BYTES)


def _const_spec(shape):
    zeros = (0,) * len(shape)
    return pl.BlockSpec(shape, lambda *_: zeros)


def _split3(x):
    hi = x.astype(jnp.bfloat16)
    r1 = x - hi.astype(jnp.float32)
    mid = r1.astype(jnp.bfloat16)
    lo = (r1 - mid.astype(jnp.float32)).astype(jnp.bfloat16)
    return hi, mid, lo


def _seg_sum(x, ones_bf16):
    hi, mid, lo = _split3(x)
    dot = functools.partial(jnp.dot, preferred_element_type=jnp.float32)
    return dot(hi, ones_bf16) + dot(mid, ones_bf16) + dot(lo, ones_bf16)


def _sigmoid(x):
    return 1.0 / (1.0 + jnp.exp(-x))


def _silu(x):
    return x * _sigmoid(x)


def _softplus(x):
    return jnp.maximum(x, 0.0) + jnp.log1p(jnp.exp(-jnp.abs(x)))


def _bdot(a, b):
    return jnp.dot(a.astype(jnp.bfloat16), b, preferred_element_type=jnp.float32)


_CONTRACT_LAST = (((1,), (1,)), ((), ()))


def _values_t(wv_t, ckvn):
    return lax.dot_general(wv_t, ckvn.astype(jnp.bfloat16), _CONTRACT_LAST,
                           preferred_element_type=jnp.float32).astype(jnp.bfloat16)


def _mod_kernel(c_ref, w_ref, b_ref, o_ref):
    a = _silu(c_ref[...])
    w = w_ref[0]
    a_hi, a_mid, _ = _split3(a)
    w_hi, w_mid, _ = _split3(w)
    dot = functools.partial(jnp.dot, preferred_element_type=jnp.float32)
    o_ref[0] = dot(a_hi, w_hi) + dot(a_hi, w_mid) + dot(a_mid, w_hi) + b_ref[0]


def _modulation(cvecs, w_mod, b_mod):
    depth, d, d3 = w_mod.shape
    rows = cvecs.shape[0]
    nblk = d3 // d
    return pl.pallas_call(
        _mod_kernel,
        grid=(depth, nblk),
        in_specs=[
            pl.BlockSpec((rows, d), lambda l, j: (0, 0)),
            pl.BlockSpec((1, d, d), lambda l, j: (l, 0, j)),
            pl.BlockSpec((1, 1, d), lambda l, j: (l, 0, j)),
        ],
        out_specs=pl.BlockSpec((1, rows, d), lambda l, j: (l, 0, j)),
        out_shape=jax.ShapeDtypeStruct((depth, rows, d3), jnp.float32),
        compiler_params=_params(("parallel", "parallel")),
        name="adaln_mod",
    )(cvecs, w_mod, b_mod.reshape(depth, 1, d3))


def _pre_kernel(rope, *refs):
    if rope:
        (x_ref, mod_ref, win_ref, lw2_ref, la2_ref, rwp_ref, wq_ref, wkv_ref, wvt_ref, qn_ref, kvn_ref,
         ones_ref, ta_ref, tb_ref, ca_ref, sb_ref,
         rwk_ref, rwe_ref, q_ref, k_ref, v_ref, gm_ref, xlgl_ref) = refs
    else:
        (x_ref, mod_ref, win_ref, lw2_ref, la2_ref, rwp_ref, wq_ref, wkv_ref, wvt_ref, qn_ref, kvn_ref,
         ones_ref,
         rwk_ref, rwe_ref, q_ref, k_ref, v_ref, gm_ref, xlgl_ref, ckvn_ref, kr_ref) = refs

    x = x_ref[0]
    shift = mod_ref[0, 0:1, :]
    scale = mod_ref[0, 1:2, :]
    mu = jnp.mean(x, axis=-1, keepdims=True)
    xc = x - mu
    var = jnp.mean(xc * xc, axis=-1, keepdims=True)
    u = xc * lax.rsqrt(var + LN_EPS) * (1.0 + scale) + shift
    h = _bdot(u, win_ref[...])

    r = h[:, 0:D_RW]
    k = h[:, D_RW:2 * D_RW]
    v = h[:, 2 * D_RW:3 * D_RW]
    g = h[:, 3 * D_RW:4 * D_RW]
    lora_w = h[:, C_LORA:C_LORA + 128]
    lora_a = h[:, C_LORA + 128:C_LORA + 256]
    wl = _bdot(jnp.tanh(lora_w), lw2_ref[...])
    al = _bdot(lora_a, la2_ref[...])
    ones = ones_ref[...]
    kk = k * rwp_ref[R_KK:R_KK + 1, :]
    kk = kk / jnp.maximum(jnp.sqrt(_seg_sum(kk * kk, ones)), 1e-12)
    bonus = _seg_sum(r * k * rwp_ref[R_RK:R_RK + 1, :], ones) * v
    ka = rwp_ref[R_KA:R_KA + 1, :]
    def put(slot, val):
        for tb in range(val.shape[0] // SCAN_STEPS):
            for half in range(D_RW // LANES):
                rwk_ref[tb, 2 * slot + half, 0] = val[tb * SCAN_STEPS:(tb + 1) * SCAN_STEPS,
                                                      half * LANES:(half + 1) * LANES]

    put(RWK_R, r)
    put(RWK_NKK, -kk)
    put(RWK_V, v)
    for d in range(2):
        wl_d = wl[:, d * D_RW:(d + 1) * D_RW] + rwp_ref[R_W0 + d:R_W0 + d + 1, :]
        decay = jnp.exp(-math.exp(-0.5) * _sigmoid(wl_d))
        a_d = _sigmoid(al[:, d * D_RW:(d + 1) * D_RW] + rwp_ref[R_A0 + d:R_A0 + d + 1, :])
        put(RWK_DIR + 3 * d, decay)
        put(RWK_DIR + 3 * d + 1, kk * a_d)
        put(RWK_DIR + 3 * d + 2, k * (1.0 + (a_d - 1.0) * ka))
    rwe_ref[0, :, 0:D_RW] = g
    rwe_ref[0, :, D_RW:2 * D_RW] = bonus

    cq = h[:, C_CQ:C_CQ + Q_RANK]
    cqn = cq * lax.rsqrt(jnp.mean(cq * cq, axis=-1, keepdims=True) + RMS_EPS) * qn_ref[...]
    ckv = h[:, C_CKV:C_CKV + KV_RANK]
    ckvn = ckv * lax.rsqrt(jnp.mean(ckv * ckv, axis=-1, keepdims=True) + RMS_EPS) * kvn_ref[...]
    qq = _bdot(cqn, wq_ref[...])
    kv = _bdot(ckvn, wkv_ref[...])
    v_ref[0] = _values_t(wvt_ref[...], ckvn)
    kra = h[:, C_KRA:C_KRA + SLOT]
    krb = h[:, C_KRB:C_KRB + SLOT]
    if rope:
        k_rope = kra * ca_ref[...] + krb * sb_ref[...]
        ta = ta_ref[...]
        tb = tb_ref[...]
    else:
        k_rope = krb
        ckvn_ref[0] = ckvn
        kr_ref[0] = kra[:, 0:MLA_ROPE]
    for hd in range(MLA_HEADS):
        lo, hi = hd * SLOT, (hd + 1) * SLOT
        if rope:
            qh = qq[:, lo:hi] * ta + qq[:, D_SLOTS + lo:D_SLOTS + hi] * tb
        else:
            qh = qq[:, lo:hi] * Q_SCALE
        q_ref[0, :, lo:hi] = qh.astype(jnp.bfloat16)
        k_ref[0, :, lo:hi] = (kv[:, lo:hi] + k_rope).astype(jnp.bfloat16)
    gm_ref[0] = h[:, C_GM:C_GM + D_MLA]
    xlgl_ref[0] = h[:, C_XL:C_XL + 2 * D_LRU]


def _pre(x, mod, wts, rope_tabs, tile):
    b, l, d = x.shape
    rope = rope_tabs is not None
    nt = l // tile
    mod_b = mod.shape[0]
    tok = lambda width: pl.BlockSpec((1, tile, width), lambda bi, ti: (bi, ti, 0))
    in_specs = [
        tok(d),
        pl.BlockSpec((1, 3, d), (lambda bi, ti: (bi, 0, 0)) if mod_b > 1 else (lambda bi, ti: (0, 0, 0))),
        _const_spec(wts["w_cat"].shape), _const_spec(wts["lw2"].shape), _const_spec(wts["la2"].shape),
        _const_spec(wts["rwp"].shape), _const_spec(wts["wq"].shape), _const_spec(wts["wkv"].shape),
        _const_spec(wts["wv_t"].shape), _const_spec(wts["qnorm"].shape), _const_spec(wts["kvnorm"].shape), _const_spec(wts["ones"].shape),
    ]
    args = [x, mod, wts["w_cat"], wts["lw2"], wts["la2"], wts["rwp"], wts["wq"], wts["wkv"], wts["wv_t"],
            wts["qnorm"], wts["kvnorm"], wts["ones"]]
    f32, bf16 = jnp.float32, jnp.bfloat16
    out_shape = [
        jax.ShapeDtypeStruct((l // SCAN_STEPS, N_RWK_CHUNKS, b, SCAN_STEPS, LANES), f32),
        jax.ShapeDtypeStruct((b, l, 2 * D_RW), f32),
        jax.ShapeDtypeStruct((b, l, D_SLOTS), bf16), jax.ShapeDtypeStruct((b, l, D_SLOTS), bf16),
        jax.ShapeDtypeStruct((b, D_MLA, l), bf16), jax.ShapeDtypeStruct((b, l, D_MLA), f32),
        jax.ShapeDtypeStruct((b, l, 2 * D_LRU), f32),
    ]
    rwk_spec = pl.BlockSpec((tile // SCAN_STEPS, N_RWK_CHUNKS, 1, SCAN_STEPS, LANES),
                            lambda bi, ti: (ti, 0, bi, 0, 0))
    vt_spec = pl.BlockSpec((1, D_MLA, tile), lambda bi, ti: (bi, 0, ti))
    out_specs = [rwk_spec, tok(2 * D_RW), tok(D_SLOTS), tok(D_SLOTS), vt_spec, tok(D_MLA), tok(2 * D_LRU)]
    if rope:
        tab = pl.BlockSpec((tile, SLOT), lambda bi, ti: (ti, 0))
        in_specs += [tab, tab, tab, tab]
        args += list(rope_tabs)
    else:
        out_shape += [jax.ShapeDtypeStruct((b, l, KV_RANK), f32), jax.ShapeDtypeStruct((b, l, MLA_ROPE), f32)]
        out_specs += [tok(KV_RANK), tok(MLA_ROPE)]
    return pl.pallas_call(
        functools.partial(_pre_kernel, rope),
        grid=(b, nt), in_specs=in_specs, out_specs=out_specs, out_shape=out_shape,
        compiler_params=_params(("parallel", "parallel")),
        name="pre_rope" if rope else "pre_ctx",
    )(*args)


KT_R, KT_W, KT_B, KT_KD = range(4)
N_KT = 4
PLANES = 2


def _scan_kernel(dl, vs, steps, *refs):
    sh_refs, dr_refs = refs[0:dl], refs[dl:2 * dl]
    s0_ref = refs[2 * dl]
    y_refs = refs[2 * dl + 1:3 * dl + 1]
    sfin_ref = refs[3 * dl + 1]
    s_scr = refs[3 * dl + 2]
    kt_scrs, vv_scrs, nkk_scrs = refs[3 * dl + 3:3 * dl + 5], refs[3 * dl + 5:3 * dl + 7], refs[3 * dl + 7:]
    g = pl.program_id(0)
    i = pl.program_id(1)
    nsub = sh_refs[0].shape[0]
    rows_per_chunk = sh_refs[0].shape[1] // RWK_SHARED_CHUNKS
    sub_steps = SCAN_STEPS
    batch = rows_per_chunk // sub_steps
    vr = HEAD_RW // vs
    lane_split = (lax.broadcasted_iota(jnp.int32, (vr, LANES), 1) // (2 * batch)) % vs

    def block_row(e, j):
        if dl == 2:
            jj = j if e == 0 else steps - 1 - j
        else:
            jj = j + g * (steps - 1 - 2 * j)
        return jj // sub_steps, jj % sub_steps

    def batch_rows(ref, chunk, pos):
        sub, t = pos
        return ref[sub, pl.ds(chunk * rows_per_chunk + t, batch, stride=sub_steps), :]

    def gather(in_refs, slot, j):
        parts = []
        for e in range(dl):
            t = block_row(e, j)
            pair = [batch_rows(in_refs[e], 2 * slot + hp, t) for hp in range(2)]
            parts += pair * vs
        return jnp.concatenate(parts, axis=0)

    @pl.when(i == 0)
    def _():
        s_scr[...] = s0_ref[0]

    def prepare_tile(kt, tile, in_refs, slot, j):
        kt[tile] = gather(in_refs, slot, j).T

    def prepare_v(j, buf):
        v_t = gather(sh_refs, RWK_V, j).T
        for p in range(PLANES):
            base = p * HEAD_RW
            vv = v_t[base:base + vr, :]
            for s in range(1, vs):
                vv = jnp.where(lane_split == s, v_t[base + s * vr:base + (s + 1) * vr, :], vv)
            vv_scrs[buf][p] = vv

    def prepare_nkk(j, buf):
        nkk_scrs[buf][...] = gather(sh_refs, RWK_NKK, jnp.minimum(j, steps - 1)).T

    def row(ref, *idx):
        return jnp.broadcast_to(ref[idx[:-1] + (pl.ds(idx[-1], 1), slice(None))], (vr, LANES))

    def tree_sum(terms):
        while len(terms) > 1:
            terms = [terms[a] + terms[a + 1] for a in range(0, len(terms), 2)]
        return terms[0]

    def emit(j, y_planes):
        zero = jnp.zeros((vr, LANES), jnp.float32)
        y_full = jnp.concatenate([jnp.where(lane_split == s, y_planes[p], zero)
                                  for p in range(PLANES) for s in range(vs)], axis=0)
        y_t = y_full.T
        for e in range(dl):
            sub, t = block_row(e, j)
            for hp in range(2):
                parts = [y_t[((e * vs + s) * 2 + hp) * batch:((e * vs + s) * 2 + hp + 1) * batch, :]
                         for s in range(vs)]
                y_refs[e][0, sub, pl.ds(hp * rows_per_chunk + t, batch, stride=sub_steps), :] = tree_sum(parts)

    for tile, in_refs, slot in ((KT_R, sh_refs, RWK_R), (KT_W, dr_refs, 0), (KT_B, dr_refs, 1), (KT_KD, dr_refs, 2)):
        prepare_tile(kt_scrs[0], tile, in_refs, slot, 0)
    prepare_v(0, 0)
    prepare_nkk(0, 0)
    prepare_nkk(1, 1)
    sa_first = tuple(
        tree_sum([tree_sum([s_scr[p, k] * row(nkk_scrs[0], p * HEAD_RW + k) for k in range(k0, HEAD_RW, 4)])
                  for k0 in range(4)])
        for p in range(PLANES))

    def step(j, cur, sa, y_prev):
        nxt = 1 - cur
        kt, vv_ref, nkk_next = kt_scrs[cur], vv_scrs[cur], nkk_scrs[nxt]
        j1 = jnp.minimum(j + 1, steps - 1)
        side_work = [
            lambda: emit(jnp.maximum(j - 1, 0), y_prev),
            lambda: prepare_tile(kt_scrs[nxt], KT_W, dr_refs, 0, j1),
            lambda: prepare_tile(kt_scrs[nxt], KT_B, dr_refs, 1, j1),
            lambda: prepare_tile(kt_scrs[nxt], KT_KD, dr_refs, 2, j1),
            lambda: prepare_tile(kt_scrs[nxt], KT_R, sh_refs, RWK_R, j1),
            lambda: prepare_v(j1, nxt),
            lambda: prepare_nkk(j + 2, cur),
        ]
        chunk = PLANES * HEAD_RW // (len(side_work) + 1)
        y_planes, sa_next = [], []
        done = 0
        for p in range(PLANES):
            base = p * HEAD_RW
            vv = vv_ref[p]
            y_acc, sa_acc, y_terms, sa_terms = None, None, [], []
            for k in range(HEAD_RW):
                s_new = s_scr[p, k] * row(kt, KT_W, base + k) + (
                    sa[p] * row(kt, KT_B, base + k) + vv * row(kt, KT_KD, base + k))
                s_scr[p, k] = s_new
                y_terms.append(s_new * row(kt, KT_R, base + k))
                sa_terms.append(s_new * row(nkk_next, base + k))
                if len(y_terms) == SUM_TREE:
                    y_sum, sa_sum = tree_sum(y_terms), tree_sum(sa_terms)
                    y_acc = y_sum if y_acc is None else y_acc + y_sum
                    sa_acc = sa_sum if sa_acc is None else sa_acc + sa_sum
                    y_terms, sa_terms = [], []
                done += 1
                if done % chunk == 0 and side_work:
                    side_work.pop(0)()
            y_planes.append(y_acc)
            sa_next.append(sa_acc)
        return tuple(sa_next), tuple(y_planes)

    def unrolled_steps(jj, carry):
        for u in range(SCAN_UNROLL):
            carry = step(SCAN_UNROLL * jj + u, u % 2, *carry)
        return carry

    zeros = tuple(jnp.zeros((vr, LANES), jnp.float32) for _ in range(PLANES))
    _, y_last = lax.fori_loop(0, steps // SCAN_UNROLL, unrolled_steps, (sa_first, zeros))
    emit(steps - 1, y_last)

    @pl.when(i == pl.num_programs(1) - 1)
    def _():
        sfin_ref[0] = s_scr[...]


def _lane_plan(batch):
    per_dir = (N_HEAD_RW // PLANES) * batch
    assert LANES % per_dir == 0
    fill = LANES // per_dir
    dl = 2 if fill >= 4 else 1
    vs = fill // dl
    assert dl * vs * per_dir == LANES and HEAD_RW % (vs * SUBLANES) == 0
    return 2 // dl, dl, vs


def _rwkv_scan(rwk, s0_lanes, plan):
    groups, dl, vs = plan
    nt, chunks, b, steps, _ = rwk.shape
    vr = HEAD_RW // vs
    rows = b * steps
    rwk_rows = rwk.reshape(nt, chunks * rows, LANES)
    nsub = max(1, min(nt, SCAN_BLOCK_BYTES // (RWK_SHARED_CHUNKS * rows * LANES * 4)))
    while nt % nsub:
        nsub -= 1
    ng = nt // nsub

    def time_block(d, ti):
        return ti + d * (ng - 1 - 2 * ti)

    def slot_dir(e, gi):
        return e if dl == 2 else gi

    blk = (nsub, RWK_SHARED_CHUNKS * rows, LANES)
    in_specs = [pl.BlockSpec(blk, functools.partial(lambda e, gi, ti: (time_block(slot_dir(e, gi), ti), 0, 0), e))
                for e in range(dl)]
    in_specs += [pl.BlockSpec(blk, functools.partial(
        lambda e, gi, ti: (time_block(slot_dir(e, gi), ti), 1 + slot_dir(e, gi), 0), e)) for e in range(dl)]
    state_spec = pl.BlockSpec((1, PLANES, HEAD_RW, vr, LANES), lambda gi, ti: (gi, 0, 0, 0, 0))
    in_specs.append(state_spec)
    y_specs = [pl.BlockSpec((1, nsub, 2 * rows, LANES), functools.partial(
        lambda e, gi, ti: (gi, time_block(slot_dir(e, gi), ti), 0, 0), e)) for e in range(dl)]
    outs = pl.pallas_call(
        functools.partial(_scan_kernel, dl, vs, nsub * steps),
        grid=(groups, ng),
        in_specs=in_specs,
        out_specs=y_specs + [state_spec],
        out_shape=[jax.ShapeDtypeStruct((groups, nt, 2 * rows, LANES), jnp.float32)] * dl
        + [jax.ShapeDtypeStruct((groups, PLANES, HEAD_RW, vr, LANES), jnp.float32)],
        scratch_shapes=[pltpu.VMEM((PLANES, HEAD_RW, vr, LANES), jnp.float32),
                        pltpu.VMEM((N_KT, LANES, LANES), jnp.float32), pltpu.VMEM((N_KT, LANES, LANES), jnp.float32),
                        pltpu.VMEM((PLANES, vr, LANES), jnp.float32), pltpu.VMEM((PLANES, vr, LANES), jnp.float32),
                        pltpu.VMEM((LANES, LANES), jnp.float32), pltpu.VMEM((LANES, LANES), jnp.float32)],
        compiler_params=_params(("parallel", "arbitrary")),
        name="rwkv_scan",
    )(*([rwk_rows] * (2 * dl)), s0_lanes)
    ys = [y.reshape(groups, nt, 2, b, steps, LANES) for y in outs[:dl]]
    return ys, outs[dl]


def _state_to_lanes(s, plan):
    g, dl, vs = plan
    b = s.shape[0]
    vr = HEAD_RW // vs
    t = s.reshape(b, g, dl, N_HEAD_RW // PLANES, PLANES, vs, vr, HEAD_RW)
    t = jnp.transpose(t, (1, 4, 7, 6, 2, 5, 3, 0))
    return t.reshape(g, PLANES, HEAD_RW, vr, LANES)


def _state_from_lanes(s, plan, b):
    g, dl, vs = plan
    vr = HEAD_RW // vs
    t = s.reshape(g, PLANES, HEAD_RW, vr, dl, vs, N_HEAD_RW // PLANES, b)
    t = jnp.transpose(t, (7, 0, 4, 6, 1, 5, 3, 2))
    return t.reshape(b, 2, N_HEAD_RW, HEAD_RW, HEAD_RW)


def _rwkv(rwk, s0):
    b = rwk.shape[2]
    plan = _lane_plan(b)
    ys, s_fin = _rwkv_scan(rwk, _state_to_lanes(s0, plan), plan)
    return ys, _state_from_lanes(s_fin, plan, b)


ROW_GROUP = 64


def _reduce_rows(x, op):
    r, c = x.shape
    if r > ROW_GROUP and r % ROW_GROUP == 0:
        x = op(x.reshape(r // ROW_GROUP, ROW_GROUP, c), axis=0)
    return op(x, axis=0, keepdims=True)


def _attn_kernel(has_ctx, *refs):
    if has_ctx:
        q_ref, k_ref, v_ref, kc_ref, vc_ref, gm_ref, o_ref = refs
    else:
        q_ref, k_ref, v_ref, gm_ref, o_ref = refs
    def scores(hd):
        sl = slice(hd * SLOT, (hd + 1) * SLOT)
        q = q_ref[0, :, sl]
        s = lax.dot_general(k_ref[0, :, sl], q, _CONTRACT_LAST, preferred_element_type=jnp.float32)
        sc = (lax.dot_general(kc_ref[0, :, sl], q, _CONTRACT_LAST, preferred_element_type=jnp.float32)
              if has_ctx else None)
        return s, sc

    nxt = scores(0)
    outs = []
    for hd in range(MLA_HEADS):
        s, sc = nxt
        if hd + 1 < MLA_HEADS:
            nxt = scores(hd + 1)
        vrows = slice(hd * MLA_V, (hd + 1) * MLA_V)
        m = _reduce_rows(s, jnp.max)
        if has_ctx:
            m = jnp.maximum(m, _reduce_rows(sc, jnp.max))
        p = jnp.exp2(s - m)
        den = _reduce_rows(p, jnp.sum)
        o = jnp.dot(v_ref[0, vrows, :], p.astype(jnp.bfloat16), preferred_element_type=jnp.float32)
        if has_ctx:
            pc = jnp.exp2(sc - m)
            den = den + _reduce_rows(pc, jnp.sum)
            o = o + jnp.dot(vc_ref[0, vrows, :], pc.astype(jnp.bfloat16), preferred_element_type=jnp.float32)
        outs.append(o * (1.0 / den))
        if hd % 2 == 1:
            cols = slice((hd // 2) * LANES, (hd // 2 + 1) * LANES)
            o_ref[0, :, cols] = jnp.concatenate(outs, axis=0).T * _silu(gm_ref[0, :, cols])
            outs = []


def _attention(q, k, v, gm, ctx_kv, tile):
    b, l, _ = q.shape
    nt = l // tile
    has_ctx = ctx_kv is not None
    tok = lambda width: pl.BlockSpec((1, tile, width), lambda bi, ti: (bi, ti, 0))
    full = lambda arr: pl.BlockSpec((1,) + arr.shape[1:], lambda bi, ti: (bi, 0, 0))
    in_specs = [tok(D_SLOTS), full(k), full(v)]
    args = [q, k, v]
    if has_ctx:
        in_specs += [full(ctx_kv[0]), full(ctx_kv[1])]
        args += list(ctx_kv)
    in_specs.append(tok(D_MLA))
    args.append(gm)
    return pl.pallas_call(
        functools.partial(_attn_kernel, has_ctx),
        grid=(b, nt), in_specs=in_specs, out_specs=tok(D_MLA),
        out_shape=jax.ShapeDtypeStruct((b, l, D_MLA), jnp.float32),
        compiler_params=_params(("parallel", "parallel")),
        name="mla_attn_ctx" if has_ctx else "mla_attn",
    )(*args)


def _ctxkv_kernel(ckv_ref, kr_ref, wkv_ref, wvt_ref, k_ref, v_ref):
    kv = _bdot(ckv_ref[0], wkv_ref[...])
    kr = kr_ref[0]
    for hd in range(MLA_HEADS):
        sl = slice(hd * SLOT, (hd + 1) * SLOT)
        k_ref[0, :, sl] = (kv[:, sl] + kr).astype(jnp.bfloat16)
    v_ref[0] = _values_t(wvt_ref[...], ckv_ref[0])


def _ctx_keyvals(ckv_ctx, kr_slot, wkv, wv_t):
    b, p, _ = ckv_ctx.shape
    blk = lambda width: pl.BlockSpec((1, p, width), lambda bi: (bi, 0, 0))
    return pl.pallas_call(
        _ctxkv_kernel, grid=(b,),
        in_specs=[blk(KV_RANK), blk(SLOT), _const_spec(wkv.shape), _const_spec(wv_t.shape)],
        out_specs=[blk(D_SLOTS), pl.BlockSpec((1, D_MLA, p), lambda bi: (bi, 0, 0))],
        out_shape=[jax.ShapeDtypeStruct((b, p, D_SLOTS), jnp.bfloat16),
                   jax.ShapeDtypeStruct((b, D_MLA, p), jnp.bfloat16)],
        compiler_params=_params(("parallel",)),
        name="ctx_keyvals",
    )(ckv_ctx, kr_slot, wkv, wv_t)


def _lru_kernel(xlgl_ref, cw_ref, cb_ref, wg_ref, bg_ref, lam_ref, h0_ref, y_ref, hfin_ref,
                a_scr, b_scr, h_scr):
    l = xlgl_ref.shape[1]
    xl = xlgl_ref[0, :, 0:D_LRU]
    t_idx = lax.broadcasted_iota(jnp.int32, (l, D_LRU), 0)
    xc = cb_ref[...] + cw_ref[2:3, :] * xl
    for j, back in ((0, 2), (1, 1), (3, -1)):
        rolled = pltpu.roll(xl, back % l, 0)
        ok = (t_idx >= back) if back > 0 else (t_idx < l + back)
        xc = xc + cw_ref[j:j + 1, :] * jnp.where(ok, rolled, 0.0)
    xcb = xc.astype(jnp.bfloat16)
    r_idx = lax.broadcasted_iota(jnp.int32, (SUBLANES, D_LRU), 0)
    nblk = l // SUBLANES
    for d in range(2):
        gr = jnp.dot(xcb, wg_ref[:, d * D_LRU:(d + 1) * D_LRU], preferred_element_type=jnp.float32)
        gr = gr + bg_ref[:, d * D_LRU:(d + 1) * D_LRU]
        gi = jnp.dot(xcb, wg_ref[:, (2 + d) * D_LRU:(3 + d) * D_LRU], preferred_element_type=jnp.float32)
        gi = gi + bg_ref[:, (2 + d) * D_LRU:(3 + d) * D_LRU]
        log_a = -LRU_C * _sigmoid(gr) * _softplus(-lam_ref[d:d + 1, :])
        a = jnp.exp(log_a)
        a_scr[d] = a
        b_scr[d] = jnp.sqrt(1.0 - a * a) * _sigmoid(gi) * xc

    def block(i, carries):
        new = []
        for d in range(2):
            blk = i if d == 0 else nblk - 1 - i
            rs = pl.ds(pl.multiple_of(blk * SUBLANES, SUBLANES), SUBLANES)
            aa = a_scr[d, rs, :]
            bb = b_scr[d, rs, :]
            for s in (1, 2, 4):
                if d == 0:
                    keep = r_idx >= s
                    sh = s
                else:
                    keep = r_idx < SUBLANES - s
                    sh = SUBLANES - s
                a_sh = jnp.where(keep, pltpu.roll(aa, sh, 0), 1.0)
                b_sh = jnp.where(keep, pltpu.roll(bb, sh, 0), 0.0)
                bb = bb + aa * b_sh
                aa = aa * a_sh
            hh = bb + aa * carries[d]
            h_scr[d, rs, :] = hh
            edge = SUBLANES - 1 if d == 0 else 0
            new.append(jnp.broadcast_to(hh[edge:edge + 1, :], (SUBLANES, D_LRU)))
        return tuple(new)

    last = lax.fori_loop(0, nblk, block,
                         tuple(jnp.broadcast_to(h0_ref[0, d:d + 1, :], (SUBLANES, D_LRU)) for d in range(2)))
    for d in range(2):
        hfin_ref[0, d:d + 1, :] = last[d][0:1, :]
    gl = xlgl_ref[0, :, D_LRU:2 * D_LRU]
    y_ref[0] = (h_scr[0] + h_scr[1]) * _silu(gl)


def _lru(xlgl, h0, wts):
    b, l, _ = xlgl.shape
    return pl.pallas_call(
        _lru_kernel, grid=(b,),
        in_specs=[
            pl.BlockSpec((1, l, 2 * D_LRU), lambda bi: (bi, 0, 0)),
            _const_spec(wts["conv_w"].shape), _const_spec(wts["conv_b"].shape),
            _const_spec(wts["wg"].shape), _const_spec(wts["bg"].shape), _const_spec(wts["lam"].shape),
            pl.BlockSpec((1, 2, D_LRU), lambda bi: (bi, 0, 0)),
        ],
        out_specs=[pl.BlockSpec((1, l, D_LRU), lambda bi: (bi, 0, 0)),
                   pl.BlockSpec((1, 2, D_LRU), lambda bi: (bi, 0, 0))],
        out_shape=[jax.ShapeDtypeStruct((b, l, D_LRU), jnp.float32),
                   jax.ShapeDtypeStruct((b, 2, D_LRU), jnp.float32)],
        scratch_shapes=[pltpu.VMEM((2, l, D_LRU), jnp.float32), pltpu.VMEM((2, l, D_LRU), jnp.float32),
                        pltpu.VMEM((2, l, D_LRU), jnp.float32)],
        compiler_params=_params(("parallel",)),
        name="rglru",
    )(xlgl, wts["conv_w"], wts["conv_b"], wts["wg"], wts["bg"], wts["lam"], h0)


def _post_kernel(alpha, x_ref, mod_ref, yf_ref, yb_ref, rwe_ref, ymla_ref, ylru_ref, wo_ref, lnx_ref, ln_ref,
                 ones_ref, o_ref):
    y = jnp.concatenate(
        [jnp.concatenate([yf_ref[0, tb, hp, 0] + yb_ref[0, tb, hp, 0] for tb in range(yf_ref.shape[1])], axis=0)
         for hp in range(D_RW // LANES)], axis=-1)
    ones = ones_ref[...]
    inv_n = 1.0 / HEAD_RW
    mu = _seg_sum(y, ones) * inv_n
    yc = y - mu
    var = _seg_sum(yc * yc, ones) * inv_n
    yn = yc * lax.rsqrt(var + GN_EPS) * lnx_ref[0:1, :] + lnx_ref[1:2, :]
    y_rw = (yn + rwe_ref[0, :, D_RW:2 * D_RW]) * _silu(rwe_ref[0, :, 0:D_RW])
    out = (_bdot(y_rw, wo_ref[0:D_RW, :]) + _bdot(ymla_ref[0], wo_ref[D_RW:D_RW + D_MLA, :])
           + _bdot(ylru_ref[0], wo_ref[D_RW + D_MLA:D_RW + D_MLA + D_LRU, :]))
    z = alpha * x_ref[0] + mod_ref[0, 2:3, :] * out
    mu_z = jnp.mean(z, axis=-1, keepdims=True)
    zc = z - mu_z
    var_z = jnp.mean(zc * zc, axis=-1, keepdims=True)
    o_ref[0] = zc * lax.rsqrt(var_z + LN_EPS) * ln_ref[0:1, :] + ln_ref[1:2, :]


def _post(x, mod, ys, rwe, ymla, ylru, wts, alpha, tile):
    b, l, d = x.shape
    nt = l // tile
    mod_b = mod.shape[0]
    tok = lambda width: pl.BlockSpec((1, tile, width), lambda bi, ti: (bi, ti, 0))
    y_fwd, y_bwd, bwd_group = (ys[0], ys[1], 0) if len(ys) == 2 else (ys[0], ys[0], 1)
    y_spec = lambda grp: pl.BlockSpec((1, tile // SCAN_STEPS, 2, 1, SCAN_STEPS, LANES),
                                      lambda bi, ti: (grp, ti, 0, bi, 0, 0))
    return pl.pallas_call(
        functools.partial(_post_kernel, alpha),
        grid=(b, nt),
        in_specs=[
            tok(d),
            pl.BlockSpec((1, 3, d), (lambda bi, ti: (bi, 0, 0)) if mod_b > 1 else (lambda bi, ti: (0, 0, 0))),
            y_spec(0), y_spec(bwd_group),
            tok(2 * D_RW), tok(D_MLA), tok(D_LRU),
            _const_spec(wts["w_out"].shape), _const_spec(wts["lnx"].shape), _const_spec(wts["ln"].shape),
            _const_spec(wts["ones"].shape),
        ],
        out_specs=tok(d),
        out_shape=jax.ShapeDtypeStruct((b, l, d), jnp.float32),
        compiler_params=_params(("parallel", "parallel")),
        name="post",
    )(x, mod, y_fwd, y_bwd, rwe, ymla, ylru, wts["w_out"], wts["lnx"], wts["ln"], wts["ones"])


def _block_diag(w):
    g, i, o = w.shape
    eye = jnp.eye(g, dtype=w.dtype)
    return jnp.einsum("gio,gh->giho", w, eye).reshape(g * i, g * o)


def _place(cols, offset, width):
    d, n = cols.shape
    return jnp.pad(cols, ((0, 0), (offset, width - offset - n)))


def _pack_layer(l, P, rope):
    bf16 = jnp.bfloat16
    w_in = P["w_in"][l]
    d = w_in.shape[0]
    o_cq = 4 * D_RW
    o_ckv = o_cq + Q_RANK
    o_kr = o_ckv + KV_RANK
    o_gm = o_kr + MLA_ROPE
    o_xl = o_gm + D_MLA
    w_kr = w_in[:, o_kr:o_kr + MLA_ROPE]
    half = MLA_ROPE // 2
    perm = np.concatenate([np.arange(half) * 2, np.arange(half) * 2 + 1])
    perm_sw = np.concatenate([np.arange(half) * 2 + 1, np.arange(half) * 2])
    if rope:
        kra = _place(w_kr[:, perm], MLA_NOPE, SLOT)
        krb = _place(w_kr[:, perm_sw], MLA_NOPE, SLOT)
    else:
        kra = _place(w_kr, 0, SLOT)
        krb = _place(w_kr, MLA_NOPE, SLOT)
    lora = jnp.concatenate([P["rw_w1"][l, 0], P["rw_w1"][l, 1], P["rw_a1"][l, 0], P["rw_a1"][l, 1]], axis=1)
    w_cat = jnp.concatenate([w_in[:, 0:o_kr], kra, krb, w_in[:, o_gm:], lora], axis=1)
    assert w_cat.shape == (d, N_CAT)

    wuq = P["mla_wuq"][l].reshape(Q_RANK, MLA_HEADS, MLA_NOPE + MLA_ROPE)
    qn, qr = wuq[..., :MLA_NOPE], wuq[..., MLA_NOPE:]
    pad_q = lambda rp: jnp.pad(jnp.concatenate([qn, rp], -1), ((0, 0), (0, 0), (0, SLOT - MLA_NOPE - MLA_ROPE)))
    if rope:
        wq_a = pad_q(qr[..., perm])
        wq_b = jnp.pad(qr[..., perm_sw], ((0, 0), (0, 0), (MLA_NOPE, SLOT - MLA_NOPE - MLA_ROPE)))
        wq = jnp.concatenate([wq_a.reshape(Q_RANK, D_SLOTS), wq_b.reshape(Q_RANK, D_SLOTS)], axis=1)
    else:
        wq = pad_q(qr).reshape(Q_RANK, D_SLOTS)
    wukv = P["mla_wukv"][l].reshape(KV_RANK, MLA_HEADS, MLA_NOPE + MLA_V)
    wk = jnp.pad(wukv[..., :MLA_NOPE], ((0, 0), (0, 0), (0, SLOT - MLA_NOPE)))
    wkv = wk.reshape(KV_RANK, D_SLOTS)
    wv_t = wukv[..., MLA_NOPE:].reshape(KV_RANK, D_MLA).T

    zeros_w2 = jnp.zeros_like(P["rw_w2"][l, 0])
    lw2 = jnp.block([[P["rw_w2"][l, 0], zeros_w2], [zeros_w2, P["rw_w2"][l, 1]]])
    la2 = jnp.block([[P["rw_a2"][l, 0], zeros_w2], [zeros_w2, P["rw_a2"][l, 1]]])
    rwp = jnp.concatenate([P["rw_w0"][l], P["rw_a0"][l], P["rw_kk"][l][None], P["rw_ka"][l][None],
                           P["rw_rk"][l].reshape(1, D_RW), jnp.zeros((1, D_RW), jnp.float32)], axis=0)
    grp = jnp.arange(D_RW) // HEAD_RW
    ones = (grp[:, None] == grp[None, :]).astype(bf16)
    wg = jnp.concatenate([_block_diag(P["lru_wa"][l, 0]), _block_diag(P["lru_wa"][l, 1]),
                          _block_diag(P["lru_wx"][l, 0]), _block_diag(P["lru_wx"][l, 1])], axis=1)
    bg = jnp.concatenate([P["lru_ba"][l, 0], P["lru_ba"][l, 1], P["lru_bx"][l, 0], P["lru_bx"][l, 1]])[None]
    return {
        "w_cat": w_cat.astype(bf16), "lw2": lw2.astype(bf16), "la2": la2.astype(bf16), "rwp": rwp,
        "wq": wq.astype(bf16), "wkv": wkv.astype(bf16), "wv_t": wv_t.astype(bf16),
        "qnorm": P["mla_qnorm"][l][None], "kvnorm": P["mla_kvnorm"][l][None], "ones": ones,
        "conv_w": P["lru_conv_w"][l], "conv_b": P["lru_conv_b"][l][None],
        "wg": wg.astype(bf16), "bg": bg, "lam": P["lru_lambda"][l],
        "w_out": P["w_out"][l].astype(bf16),
        "lnx": jnp.stack([P["rw_lnx_g"][l], P["rw_lnx_b"][l]]),
        "ln": jnp.stack([P["ln_g"][l], P["ln_b"][l]]),
        "perm": perm,
    }


def _rope_tables(l):
    rows = l // GRID_W
    row = jnp.repeat(jnp.arange(rows), GRID_W).astype(jnp.float32)
    col = jnp.tile(jnp.arange(GRID_W), rows).astype(jnp.float32)
    pairs = MLA_ROPE // 4
    inv = ROPE_BASE ** (-jnp.arange(pairs, dtype=jnp.float32) / pairs)
    ang = jnp.concatenate([row[:, None] * inv, col[:, None] * inv], axis=-1)
    cos, sin = jnp.cos(ang), jnp.sin(ang)
    tail = jnp.zeros((l, SLOT - MLA_NOPE - MLA_ROPE), jnp.float32)
    nope0 = jnp.zeros((l, MLA_NOPE), jnp.float32)
    ca = jnp.concatenate([nope0, cos, cos, tail], axis=1)
    sb = jnp.concatenate([nope0, -sin, sin, tail], axis=1)
    ta = jnp.concatenate([jnp.ones((l, MLA_NOPE), jnp.float32), cos, cos, tail], axis=1) * Q_SCALE
    tb = sb * Q_SCALE
    return ta, tb, ca, sb


def _layer(x, mod, wts, alpha, ctx, rope_tabs):
    b, l, _ = x.shape
    tile = min(l, ATTN_TILE)
    proj_tile = min(l, PROJ_TILE)
    outs = _pre(x, mod, wts, rope_tabs, proj_tile)
    rwk, rwe, q, k, v, gm, xlgl = outs[:7]
    if ctx is None:
        s0 = jnp.zeros((b, 2, N_HEAD_RW, HEAD_RW, HEAD_RW), jnp.float32)
        h0 = jnp.zeros((b, 2, D_LRU), jnp.float32)
        ctx_kv = None
    else:
        s0, ckv_ctx, kr_ctx, h0 = ctx
        kr_slot = jnp.pad(kr_ctx[..., wts["perm"]], ((0, 0), (0, 0), (MLA_NOPE, SLOT - MLA_NOPE - MLA_ROPE)))
        ctx_kv = _ctx_keyvals(ckv_ctx, kr_slot, wts["wkv"], wts["wv_t"])
    yrw, s_fin = _rwkv(rwk, s0)
    ymla = _attention(q, k, v, gm, ctx_kv, tile)
    ylru, h_fin = _lru(xlgl, h0, wts)
    x_new = _post(x, mod, yrw, rwe, ymla, ylru, wts, alpha, proj_tile)
    extras = (s_fin, outs[7], outs[8], h_fin) if ctx is None else None
    return x_new, extras


def kernel(x_prompt, x_sample, state_rwkv, cache_mla_ckv, cache_mla_krope, state_lru, c, c_ctx, w_mod, b_mod,
           w_in, rw_w0, rw_w1, rw_w2, rw_a0, rw_a1, rw_a2, rw_kk, rw_ka, rw_rk, rw_lnx_g, rw_lnx_b, mla_qnorm,
           mla_wuq, mla_kvnorm, mla_wukv, lru_conv_w, lru_conv_b, lru_wa, lru_ba, lru_wx, lru_bx, lru_lambda,
           w_out, ln_g, ln_b):
    P = dict(w_in=w_in, rw_w0=rw_w0, rw_w1=rw_w1, rw_w2=rw_w2, rw_a0=rw_a0, rw_a1=rw_a1, rw_a2=rw_a2,
             rw_kk=rw_kk, rw_ka=rw_ka, rw_rk=rw_rk, rw_lnx_g=rw_lnx_g, rw_lnx_b=rw_lnx_b, mla_qnorm=mla_qnorm,
             mla_wuq=mla_wuq, mla_kvnorm=mla_kvnorm, mla_wukv=mla_wukv, lru_conv_w=lru_conv_w,
             lru_conv_b=lru_conv_b, lru_wa=lru_wa, lru_ba=lru_ba, lru_wx=lru_wx, lru_bx=lru_bx,
             lru_lambda=lru_lambda, w_out=w_out, ln_g=ln_g, ln_b=ln_b)
    depth = w_in.shape[0]
    alpha = (2 * depth) ** 0.25
    dec_b, dec_l, d = x_sample.shape

    rows = -(-(1 + dec_b) // SUBLANES) * SUBLANES
    cvecs = jnp.concatenate([c_ctx[None, :], c, jnp.zeros((rows - 1 - dec_b, d), jnp.float32)], axis=0)
    mod_all = _modulation(cvecs, w_mod, b_mod).reshape(depth, rows, 3, d)
    rope_tabs = _rope_tables(dec_l)

    xp, xs = x_prompt, x_sample
    rw_states, ckvs, krs, lru_states = [], [], [], []
    for l in range(depth):
        xp, (s_l, ckv_l, kr_l, h_l) = _layer(xp, mod_all[l, 0:1], _pack_layer(l, P, rope=False), alpha, None, None)
        rw_states.append(s_l)
        ckvs.append(ckv_l)
        krs.append(kr_l)
        lru_states.append(h_l)
    for l in range(depth):
        ctx = (state_rwkv[:, l], cache_mla_ckv[:, l], cache_mla_krope[:, l], state_lru[:, l])
        xs, _ = _layer(xs, mod_all[l, 1:1 + dec_b], _pack_layer(l, P, rope=True), alpha, ctx, rope_tabs)
    return (xp, xs, jnp.stack(rw_states, axis=1), jnp.stack(ckvs, axis=1), jnp.stack(krs, axis=1),
            jnp.stack(lru_states, axis=1))
```

```python
import functools
import math

import jax
import jax.numpy as jnp
import numpy as np
from jax import lax
from jax.experimental import pallas as pl
from jax.experimental.pallas import tpu as pltpu

LANES = 128
SUBLANES = 8
VMEM_LIMIT_BYTES = 56 * 1024 * 1024

N_HEAD_RW = 4
HEAD_RW = 64
D_RW = N_HEAD_RW * HEAD_RW
MLA_HEADS = 8
MLA_NOPE = 64
MLA_ROPE = 32
MLA_V = 64
D_MLA = MLA_HEADS * MLA_V
Q_RANK = 256
KV_RANK = 128
D_LRU = 256
LRU_BLOCKS = 4
LRU_BLOCK = 64
GRID_W = 64
ROPE_BASE = 10000.0
SOFTMAX_SCALE = (MLA_NOPE + MLA_ROPE) ** -0.5
Q_SCALE = SOFTMAX_SCALE * math.log2(math.e)
LRU_C = 8.0
GN_EPS = 64e-5
LN_EPS = 1e-5
RMS_EPS = 1e-6
ATTN_TILE = 256
PROJ_TILE = 512
SLOT = LANES
D_SLOTS = MLA_HEADS * SLOT

C_RKVG = 0
C_CQ = C_RKVG + 4 * D_RW
C_CKV = C_CQ + Q_RANK
C_KRA = C_CKV + KV_RANK
C_KRB = C_KRA + SLOT
C_GM = C_KRB + SLOT
C_XL = C_GM + D_MLA
C_GL = C_XL + D_LRU
C_LORA = C_GL + D_LRU
N_CAT = C_LORA + 4 * 64

R_W0, R_A0, R_KK, R_KA, R_RK = 0, 2, 4, 5, 6

SCAN_STEPS = 16
SCAN_BLOCK_BYTES = 13 * 256 * 1024
SCAN_UNROLL = 8
SUM_TREE = 8
RWK_R, RWK_NKK, RWK_V, RWK_DIR = 0, 1, 2, 3
N_RWK_CHUNKS = 2 * 9
RWK_SHARED_CHUNKS = 2 * 3


def _params(sem):
    return pltpu.CompilerParams(dimension_semantics=sem, vmem_limit_bytes=VMEM_LIMIT_BYTES)


def _const_spec(shape):
    zeros = (0,) * len(shape)
    return pl.BlockSpec(shape, lambda *_: zeros)


def _split3(x):
    hi = x.astype(jnp.bfloat16)
    r1 = x - hi.astype(jnp.float32)
    mid = r1.astype(jnp.bfloat16)
    lo = (r1 - mid.astype(jnp.float32)).astype(jnp.bfloat16)
    return hi, mid, lo


def _seg_sum(x, ones_bf16):
    hi, mid, lo = _split3(x)
    dot = functools.partial(jnp.dot, preferred_element_type=jnp.float32)
    return dot(hi, ones_bf16) + dot(mid, ones_bf16) + dot(lo, ones_bf16)


def _sigmoid(x):
    return 1.0 / (1.0 + jnp.exp(-x))


def _silu(x):
    return x * _sigmoid(x)


def _softplus(x):
    return jnp.maximum(x, 0.0) + jnp.log1p(jnp.exp(-jnp.abs(x)))


def _bdot(a, b):
    return jnp.dot(a.astype(jnp.bfloat16), b, preferred_element_type=jnp.float32)


_CONTRACT_LAST = (((1,), (1,)), ((), ()))


def _values_t(wv_t, ckvn):
    return lax.dot_general(wv_t, ckvn.astype(jnp.bfloat16), _CONTRACT_LAST,
                           preferred_element_type=jnp.float32).astype(jnp.bfloat16)


def _mod_kernel(c_ref, w_ref, b_ref, o_ref):
    a = _silu(c_ref[...])
    w = w_ref[0]
    a_hi, a_mid, _ = _split3(a)
    w_hi, w_mid, _ = _split3(w)
    dot = functools.partial(jnp.dot, preferred_element_type=jnp.float32)
    o_ref[0] = dot(a_hi, w_hi) + dot(a_hi, w_mid) + dot(a_mid, w_hi) + b_ref[0]


def _modulation(cvecs, w_mod, b_mod):
    depth, d, d3 = w_mod.shape
    rows = cvecs.shape[0]
    nblk = d3 // d
    return pl.pallas_call(
        _mod_kernel,
        grid=(depth, nblk),
        in_specs=[
            pl.BlockSpec((rows, d), lambda l, j: (0, 0)),
            pl.BlockSpec((1, d, d), lambda l, j: (l, 0, j)),
            pl.BlockSpec((1, 1, d), lambda l, j: (l, 0, j)),
        ],
        out_specs=pl.BlockSpec((1, rows, d), lambda l, j: (l, 0, j)),
        out_shape=jax.ShapeDtypeStruct((depth, rows, d3), jnp.float32),
        compiler_params=_params(("parallel", "parallel")),
        name="adaln_mod",
    )(cvecs, w_mod, b_mod.reshape(depth, 1, d3))


def _pre_kernel(rope, *refs):
    if rope:
        (x_ref, mod_ref, win_ref, lw2_ref, la2_ref, rwp_ref, wq_ref, wkv_ref, wvt_ref, qn_ref, kvn_ref,
         ones_ref, ta_ref, tb_ref, ca_ref, sb_ref,
         rwk_ref, rwe_ref, q_ref, k_ref, v_ref, gm_ref, xlgl_ref) = refs
    else:
        (x_ref, mod_ref, win_ref, lw2_ref, la2_ref, rwp_ref, wq_ref, wkv_ref, wvt_ref, qn_ref, kvn_ref,
         ones_ref,
         rwk_ref, rwe_ref, q_ref, k_ref, v_ref, gm_ref, xlgl_ref, ckvn_ref, kr_ref) = refs

    x = x_ref[0]
    shift = mod_ref[0, 0:1, :]
    scale = mod_ref[0, 1:2, :]
    mu = jnp.mean(x, axis=-1, keepdims=True)
    xc = x - mu
    var = jnp.mean(xc * xc, axis=-1, keepdims=True)
    u = xc * lax.rsqrt(var + LN_EPS) * (1.0 + scale) + shift
    h = _bdot(u, win_ref[...])

    r = h[:, 0:D_RW]
    k = h[:, D_RW:2 * D_RW]
    v = h[:, 2 * D_RW:3 * D_RW]
    g = h[:, 3 * D_RW:4 * D_RW]
    lora_w = h[:, C_LORA:C_LORA + 128]
    lora_a = h[:, C_LORA + 128:C_LORA + 256]
    wl = _bdot(jnp.tanh(lora_w), lw2_ref[...])
    al = _bdot(lora_a, la2_ref[...])
    ones = ones_ref[...]
    kk = k * rwp_ref[R_KK:R_KK + 1, :]
    kk = kk / jnp.maximum(jnp.sqrt(_seg_sum(kk * kk, ones)), 1e-12)
    bonus = _seg_sum(r * k * rwp_ref[R_RK:R_RK + 1, :], ones) * v
    ka = rwp_ref[R_KA:R_KA + 1, :]
    def put(slot, val):
        for tb in range(val.shape[0] // SCAN_STEPS):
            for half in range(D_RW // LANES):
                rwk_ref[tb, 2 * slot + half, 0] = val[tb * SCAN_STEPS:(tb + 1) * SCAN_STEPS,
                                                      half * LANES:(half + 1) * LANES]

    put(RWK_R, r)
    put(RWK_NKK, -kk)
    put(RWK_V, v)
    for d in range(2):
        wl_d = wl[:, d * D_RW:(d + 1) * D_RW] + rwp_ref[R_W0 + d:R_W0 + d + 1, :]
        decay = jnp.exp(-math.exp(-0.5) * _sigmoid(wl_d))
        a_d = _sigmoid(al[:, d * D_RW:(d + 1) * D_RW] + rwp_ref[R_A0 + d:R_A0 + d + 1, :])
        put(RWK_DIR + 3 * d, decay)
        put(RWK_DIR + 3 * d + 1, kk * a_d)
        put(RWK_DIR + 3 * d + 2, k * (1.0 + (a_d - 1.0) * ka))
    rwe_ref[0, :, 0:D_RW] = g
    rwe_ref[0, :, D_RW:2 * D_RW] = bonus

    cq = h[:, C_CQ:C_CQ + Q_RANK]
    cqn = cq * lax.rsqrt(jnp.mean(cq * cq, axis=-1, keepdims=True) + RMS_EPS) * qn_ref[...]
    ckv = h[:, C_CKV:C_CKV + KV_RANK]
    ckvn = ckv * lax.rsqrt(jnp.mean(ckv * ckv, axis=-1, keepdims=True) + RMS_EPS) * kvn_ref[...]
    qq = _bdot(cqn, wq_ref[...])
    kv = _bdot(ckvn, wkv_ref[...])
    v_ref[0] = _values_t(wvt_ref[...], ckvn)
    kra = h[:, C_KRA:C_KRA + SLOT]
    krb = h[:, C_KRB:C_KRB + SLOT]
    if rope:
        k_rope = kra * ca_ref[...] + krb * sb_ref[...]
        ta = ta_ref[...]
        tb = tb_ref[...]
    else:
        k_rope = krb
        ckvn_ref[0] = ckvn
        kr_ref[0] = kra[:, 0:MLA_ROPE]
    for hd in range(MLA_HEADS):
        lo, hi = hd * SLOT, (hd + 1) * SLOT
        if rope:
            qh = qq[:, lo:hi] * ta + qq[:, D_SLOTS + lo:D_SLOTS + hi] * tb
        else:
            qh = qq[:, lo:hi] * Q_SCALE
        q_ref[0, :, lo:hi] = qh.astype(jnp.bfloat16)
        k_ref[0, :, lo:hi] = (kv[:, lo:hi] + k_rope).astype(jnp.bfloat16)
    gm_ref[0] = h[:, C_GM:C_GM + D_MLA]
    xlgl_ref[0] = h[:, C_XL:C_XL + 2 * D_LRU]


def _pre(x, mod, wts, rope_tabs, tile):
    b, l, d = x.shape
    rope = rope_tabs is not None
    nt = l // tile
    mod_b = mod.shape[0]
    tok = lambda width: pl.BlockSpec((1, tile, width), lambda bi, ti: (bi, ti, 0))
    in_specs = [
        tok(d),
        pl.BlockSpec((1, 3, d), (lambda bi, ti: (bi, 0, 0)) if mod_b > 1 else (lambda bi, ti: (0, 0, 0))),
        _const_spec(wts["w_cat"].shape), _const_spec(wts["lw2"].shape), _const_spec(wts["la2"].shape),
        _const_spec(wts["rwp"].shape), _const_spec(wts["wq"].shape), _const_spec(wts["wkv"].shape),
        _const_spec(wts["wv_t"].shape), _const_spec(wts["qnorm"].shape), _const_spec(wts["kvnorm"].shape), _const_spec(wts["ones"].shape),
    ]
    args = [x, mod, wts["w_cat"], wts["lw2"], wts["la2"], wts["rwp"], wts["wq"], wts["wkv"], wts["wv_t"],
            wts["qnorm"], wts["kvnorm"], wts["ones"]]
    f32, bf16 = jnp.float32, jnp.bfloat16
    out_shape = [
        jax.ShapeDtypeStruct((l // SCAN_STEPS, N_RWK_CHUNKS, b, SCAN_STEPS, LANES), f32),
        jax.ShapeDtypeStruct((b, l, 2 * D_RW), f32),
        jax.ShapeDtypeStruct((b, l, D_SLOTS), bf16), jax.ShapeDtypeStruct((b, l, D_SLOTS), bf16),
        jax.ShapeDtypeStruct((b, D_MLA, l), bf16), jax.ShapeDtypeStruct((b, l, D_MLA), f32),
        jax.ShapeDtypeStruct((b, l, 2 * D_LRU), f32),
    ]
    rwk_spec = pl.BlockSpec((tile // SCAN_STEPS, N_RWK_CHUNKS, 1, SCAN_STEPS, LANES),
                            lambda bi, ti: (ti, 0, bi, 0, 0))
    vt_spec = pl.BlockSpec((1, D_MLA, tile), lambda bi, ti: (bi, 0, ti))
    out_specs = [rwk_spec, tok(2 * D_RW), tok(D_SLOTS), tok(D_SLOTS), vt_spec, tok(D_MLA), tok(2 * D_LRU)]
    if rope:
        tab = pl.BlockSpec((tile, SLOT), lambda bi, ti: (ti, 0))
        in_specs += [tab, tab, tab, tab]
        args += list(rope_tabs)
    else:
        out_shape += [jax.ShapeDtypeStruct((b, l, KV_RANK), f32), jax.ShapeDtypeStruct((b, l, MLA_ROPE), f32)]
        out_specs += [tok(KV_RANK), tok(MLA_ROPE)]
    return pl.pallas_call(
        functools.partial(_pre_kernel, rope),
        grid=(b, nt), in_specs=in_specs, out_specs=out_specs, out_shape=out_shape,
        compiler_params=_params(("parallel", "parallel")),
        name="pre_rope" if rope else "pre_ctx",
    )(*args)


KT_R, KT_W, KT_B, KT_KD = range(4)
N_KT = 4
PLANES = 2


def _scan_kernel(dl, vs, steps, *refs):
    sh_refs, dr_refs = refs[0:dl], refs[dl:2 * dl]
    s0_ref = refs[2 * dl]
    y_refs = refs[2 * dl + 1:3 * dl + 1]
    sfin_ref = refs[3 * dl + 1]
    s_scr = refs[3 * dl + 2]
    kt_scrs, vv_scrs, nkk_scrs = refs[3 * dl + 3:3 * dl + 5], refs[3 * dl + 5:3 * dl + 7], refs[3 * dl + 7:]
    g = pl.program_id(0)
    i = pl.program_id(1)
    nsub = sh_refs[0].shape[0]
    rows_per_chunk = sh_refs[0].shape[1] // RWK_SHARED_CHUNKS
    sub_steps = SCAN_STEPS
    batch = rows_per_chunk // sub_steps
    vr = HEAD_RW // vs
    lane_split = (lax.broadcasted_iota(jnp.int32, (vr, LANES), 1) // (2 * batch)) % vs

    def block_row(e, j):
        if dl == 2:
            jj = j if e == 0 else steps - 1 - j
        else:
            jj = j + g * (steps - 1 - 2 * j)
        return jj // sub_steps, jj % sub_steps

    def batch_rows(ref, chunk, pos):
        sub, t = pos
        return ref[sub, pl.ds(chunk * rows_per_chunk + t, batch, stride=sub_steps), :]

    def gather(in_refs, slot, j):
        parts = []
        for e in range(dl):
            t = block_row(e, j)
            pair = [batch_rows(in_refs[e], 2 * slot + hp, t) for hp in range(2)]
            parts += pair * vs
        return jnp.concatenate(parts, axis=0)

    @pl.when(i == 0)
    def _():
        s_scr[...] = s0_ref[0]

    def prepare_tile(kt, tile, in_refs, slot, j):
        kt[tile] = gather(in_refs, slot, j).T

    def prepare_v(j, buf):
        v_t = gather(sh_refs, RWK_V, j).T
        for p in range(PLANES):
            base = p * HEAD_RW
            vv = v_t[base:base + vr, :]
            for s in range(1, vs):
                vv = jnp.where(lane_split == s, v_t[base + s * vr:base + (s + 1) * vr, :], vv)
            vv_scrs[buf][p] = vv

    def prepare_nkk(j, buf):
        nkk_scrs[buf][...] = gather(sh_refs, RWK_NKK, jnp.minimum(j, steps - 1)).T

    def row(ref, *idx):
        return jnp.broadcast_to(ref[idx[:-1] + (pl.ds(idx[-1], 1), slice(None))], (vr, LANES))

    def tree_sum(terms):
        while len(terms) > 1:
            terms = [terms[a] + terms[a + 1] for a in range(0, len(terms), 2)]
        return terms[0]

    def emit(j, y_planes):
        zero = jnp.zeros((vr, LANES), jnp.float32)
        y_full = jnp.concatenate([jnp.where(lane_split == s, y_planes[p], zero)
                                  for p in range(PLANES) for s in range(vs)], axis=0)
        y_t = y_full.T
        for e in range(dl):
            sub, t = block_row(e, j)
            for hp in range(2):
                parts = [y_t[((e * vs + s) * 2 + hp) * batch:((e * vs + s) * 2 + hp + 1) * batch, :]
                         for s in range(vs)]
                y_refs[e][0, sub, pl.ds(hp * rows_per_chunk + t, batch, stride=sub_steps), :] = tree_sum(parts)

    for tile, in_refs, slot in ((KT_R, sh_refs, RWK_R), (KT_W, dr_refs, 0), (KT_B, dr_refs, 1), (KT_KD, dr_refs, 2)):
        prepare_tile(kt_scrs[0], tile, in_refs, slot, 0)
    prepare_v(0, 0)
    prepare_nkk(0, 0)
    prepare_nkk(1, 1)
    sa_first = tuple(
        tree_sum([tree_sum([s_scr[p, k] * row(nkk_scrs[0], p * HEAD_RW + k) for k in range(k0, HEAD_RW, 4)])
                  for k0 in range(4)])
        for p in range(PLANES))

    def step(j, cur, sa, y_prev):
        nxt = 1 - cur
        kt, vv_ref, nkk_next = kt_scrs[cur], vv_scrs[cur], nkk_scrs[nxt]
        j1 = jnp.minimum(j + 1, steps - 1)
        side_work = [
            lambda: emit(jnp.maximum(j - 1, 0), y_prev),
            lambda: prepare_tile(kt_scrs[nxt], KT_W, dr_refs, 0, j1),
            lambda: prepare_tile(kt_scrs[nxt], KT_B, dr_refs, 1, j1),
            lambda: prepare_tile(kt_scrs[nxt], KT_KD, dr_refs, 2, j1),
            lambda: prepare_tile(kt_scrs[nxt], KT_R, sh_refs, RWK_R, j1),
            lambda: prepare_v(j1, nxt),
            lambda: prepare_nkk(j + 2, cur),
        ]
        chunk = PLANES * HEAD_RW // (2 * (len(side_work) + 1))
        y_planes, sa_next = [], []
        done = 0
        for p in range(PLANES):
            base = p * HEAD_RW
            vv = vv_ref[p]
            y_acc, sa_acc, y_terms, sa_terms = None, None, [], []
            for k in range(HEAD_RW):
                s_new = s_scr[p, k] * row(kt, KT_W, base + k) + (
                    sa[p] * row(kt, KT_B, base + k) + vv * row(kt, KT_KD, base + k))
                s_scr[p, k] = s_new
                y_terms.append(s_new * row(kt, KT_R, base + k))
                sa_terms.append(s_new * row(nkk_next, base + k))
                if len(y_terms) == SUM_TREE:
                    y_sum, sa_sum = tree_sum(y_terms), tree_sum(sa_terms)
                    y_acc = y_sum if y_acc is None else y_acc + y_sum
                    sa_acc = sa_sum if sa_acc is None else sa_acc + sa_sum
                    y_terms, sa_terms = [], []
                done += 1
                if done % chunk == 0 and side_work:
                    side_work.pop(0)()
            y_planes.append(y_acc)
            sa_next.append(sa_acc)
        return tuple(sa_next), tuple(y_planes)

    def unrolled_steps(jj, carry):
        for u in range(SCAN_UNROLL):
            carry = step(SCAN_UNROLL * jj + u, u % 2, *carry)
        return carry

    zeros = tuple(jnp.zeros((vr, LANES), jnp.float32) for _ in range(PLANES))
    _, y_last = lax.fori_loop(0, steps // SCAN_UNROLL, unrolled_steps, (sa_first, zeros))
    emit(steps - 1, y_last)

    @pl.when(i == pl.num_programs(1) - 1)
    def _():
        sfin_ref[0] = s_scr[...]


def _lane_plan(batch):
    per_dir = (N_HEAD_RW // PLANES) * batch
    assert LANES % per_dir == 0
    fill = LANES // per_dir
    dl = 2 if fill >= 4 else 1
    vs = fill // dl
    assert dl * vs * per_dir == LANES and HEAD_RW % (vs * SUBLANES) == 0
    return 2 // dl, dl, vs


def _rwkv_scan(rwk, s0_lanes, plan):
    groups, dl, vs = plan
    nt, chunks, b, steps, _ = rwk.shape
    vr = HEAD_RW // vs
    rows = b * steps
    rwk_rows = rwk.reshape(nt, chunks * rows, LANES)
    nsub = max(1, min(nt, SCAN_BLOCK_BYTES // (RWK_SHARED_CHUNKS * rows * LANES * 4)))
    while nt % nsub:
        nsub -= 1
    ng = nt // nsub

    def time_block(d, ti):
        return ti + d * (ng - 1 - 2 * ti)

    def slot_dir(e, gi):
        return e if dl == 2 else gi

    blk = (nsub, RWK_SHARED_CHUNKS * rows, LANES)
    in_specs = [pl.BlockSpec(blk, functools.partial(lambda e, gi, ti: (time_block(slot_dir(e, gi), ti), 0, 0), e))
                for e in range(dl)]
    in_specs += [pl.BlockSpec(blk, functools.partial(
        lambda e, gi, ti: (time_block(slot_dir(e, gi), ti), 1 + slot_dir(e, gi), 0), e)) for e in range(dl)]
    state_spec = pl.BlockSpec((1, PLANES, HEAD_RW, vr, LANES), lambda gi, ti: (gi, 0, 0, 0, 0))
    in_specs.append(state_spec)
    y_specs = [pl.BlockSpec((1, nsub, 2 * rows, LANES), functools.partial(
        lambda e, gi, ti: (gi, time_block(slot_dir(e, gi), ti), 0, 0), e)) for e in range(dl)]
    outs = pl.pallas_call(
        functools.partial(_scan_kernel, dl, vs, nsub * steps),
        grid=(groups, ng),
        in_specs=in_specs,
        out_specs=y_specs + [state_spec],
        out_shape=[jax.ShapeDtypeStruct((groups, nt, 2 * rows, LANES), jnp.float32)] * dl
        + [jax.ShapeDtypeStruct((groups, PLANES, HEAD_RW, vr, LANES), jnp.float32)],
        scratch_shapes=[pltpu.VMEM((PLANES, HEAD_RW, vr, LANES), jnp.float32),
                        pltpu.VMEM((N_KT, LANES, LANES), jnp.float32), pltpu.VMEM((N_KT, LANES, LANES), jnp.float32),
                        pltpu.VMEM((PLANES, vr, LANES), jnp.float32), pltpu.VMEM((PLANES, vr, LANES), jnp.float32),
                        pltpu.VMEM((LANES, LANES), jnp.float32), pltpu.VMEM((LANES, LANES), jnp.float32)],
        compiler_params=_params(("parallel", "arbitrary")),
        name="rwkv_scan",
    )(*([rwk_rows] * (2 * dl)), s0_lanes)
    ys = [y.reshape(groups, nt, 2, b, steps, LANES) for y in outs[:dl]]
    return ys, outs[dl]


def _state_to_lanes(s, plan):
    g, dl, vs = plan
    b = s.shape[0]
    vr = HEAD_RW // vs
    t = s.reshape(b, g, dl, N_HEAD_RW // PLANES, PLANES, vs, vr, HEAD_RW)
    t = jnp.transpose(t, (1, 4, 7, 6, 2, 5, 3, 0))
    return t.reshape(g, PLANES, HEAD_RW, vr, LANES)


def _state_from_lanes(s, plan, b):
    g, dl, vs = plan
    vr = HEAD_RW // vs
    t = s.reshape(g, PLANES, HEAD_RW, vr, dl, vs, N_HEAD_RW // PLANES, b)
    t = jnp.transpose(t, (7, 0, 4, 6, 1, 5, 3, 2))
    return t.reshape(b, 2, N_HEAD_RW, HEAD_RW, HEAD_RW)


def _rwkv(rwk, s0):
    b = rwk.shape[2]
    plan = _lane_plan(b)
    ys, s_fin = _rwkv_scan(rwk, _state_to_lanes(s0, plan), plan)
    return ys, _state_from_lanes(s_fin, plan, b)


ROW_GROUP = 64


def _reduce_rows(x, op):
    r, c = x.shape
    if r > ROW_GROUP and r % ROW_GROUP == 0:
        x = op(x.reshape(r // ROW_GROUP, ROW_GROUP, c), axis=0)
    return op(x, axis=0, keepdims=True)


def _attn_kernel(has_ctx, *refs):
    if has_ctx:
        q_ref, k_ref, v_ref, kc_ref, vc_ref, gm_ref, o_ref = refs
    else:
        q_ref, k_ref, v_ref, gm_ref, o_ref = refs
    def scores(hd):
        sl = slice(hd * SLOT, (hd + 1) * SLOT)
        q = q_ref[0, :, sl]
        s = lax.dot_general(k_ref[0, :, sl], q, _CONTRACT_LAST, preferred_element_type=jnp.float32)
        sc = (lax.dot_general(kc_ref[0, :, sl], q, _CONTRACT_LAST, preferred_element_type=jnp.float32)
              if has_ctx else None)
        return s, sc

    nxt = scores(0)
    outs = []
    for hd in range(MLA_HEADS):
        s, sc = nxt
        if hd + 1 < MLA_HEADS:
            nxt = scores(hd + 1)
        vrows = slice(hd * MLA_V, (hd + 1) * MLA_V)
        m = _reduce_rows(s, jnp.max)
        if has_ctx:
            m = jnp.maximum(m, _reduce_rows(sc, jnp.max))
        p = jnp.exp2(s - m)
        den = _reduce_rows(p, jnp.sum)
        o = jnp.dot(v_ref[0, vrows, :], p.astype(jnp.bfloat16), preferred_element_type=jnp.float32)
        if has_ctx:
            pc = jnp.exp2(sc - m)
            den = den + _reduce_rows(pc, jnp.sum)
            o = o + jnp.dot(vc_ref[0, vrows, :], pc.astype(jnp.bfloat16), preferred_element_type=jnp.float32)
        outs.append(o * (1.0 / den))
        if hd % 2 == 1:
            cols = slice((hd // 2) * LANES, (hd // 2 + 1) * LANES)
            o_ref[0, :, cols] = jnp.concatenate(outs, axis=0).T * _silu(gm_ref[0, :, cols])
            outs = []


def _attention(q, k, v, gm, ctx_kv, tile):
    b, l, _ = q.shape
    nt = l // tile
    has_ctx = ctx_kv is not None
    tok = lambda width: pl.BlockSpec((1, tile, width), lambda bi, ti: (bi, ti, 0))
    full = lambda arr: pl.BlockSpec((1,) + arr.shape[1:], lambda bi, ti: (bi, 0, 0))
    in_specs = [tok(D_SLOTS), full(k), full(v)]
    args = [q, k, v]
    if has_ctx:
        in_specs += [full(ctx_kv[0]), full(ctx_kv[1])]
        args += list(ctx_kv)
    in_specs.append(tok(D_MLA))
    args.append(gm)
    return pl.pallas_call(
        functools.partial(_attn_kernel, has_ctx),
        grid=(b, nt), in_specs=in_specs, out_specs=tok(D_MLA),
        out_shape=jax.ShapeDtypeStruct((b, l, D_MLA), jnp.float32),
        compiler_params=_params(("parallel", "parallel")),
        name="mla_attn_ctx" if has_ctx else "mla_attn",
    )(*args)


def _ctxkv_kernel(ckv_ref, kr_ref, wkv_ref, wvt_ref, k_ref, v_ref):
    kv = _bdot(ckv_ref[0], wkv_ref[...])
    kr = kr_ref[0]
    for hd in range(MLA_HEADS):
        sl = slice(hd * SLOT, (hd + 1) * SLOT)
        k_ref[0, :, sl] = (kv[:, sl] + kr).astype(jnp.bfloat16)
    v_ref[0] = _values_t(wvt_ref[...], ckv_ref[0])


def _ctx_keyvals(ckv_ctx, kr_slot, wkv, wv_t):
    b, p, _ = ckv_ctx.shape
    blk = lambda width: pl.BlockSpec((1, p, width), lambda bi: (bi, 0, 0))
    return pl.pallas_call(
        _ctxkv_kernel, grid=(b,),
        in_specs=[blk(KV_RANK), blk(SLOT), _const_spec(wkv.shape), _const_spec(wv_t.shape)],
        out_specs=[blk(D_SLOTS), pl.BlockSpec((1, D_MLA, p), lambda bi: (bi, 0, 0))],
        out_shape=[jax.ShapeDtypeStruct((b, p, D_SLOTS), jnp.bfloat16),
                   jax.ShapeDtypeStruct((b, D_MLA, p), jnp.bfloat16)],
        compiler_params=_params(("parallel",)),
        name="ctx_keyvals",
    )(ckv_ctx, kr_slot, wkv, wv_t)


def _lru_kernel(xlgl_ref, cw_ref, cb_ref, wg_ref, bg_ref, lam_ref, h0_ref, y_ref, hfin_ref,
                a_scr, b_scr, h_scr):
    l = xlgl_ref.shape[1]
    xl = xlgl_ref[0, :, 0:D_LRU]
    t_idx = lax.broadcasted_iota(jnp.int32, (l, D_LRU), 0)
    xc = cb_ref[...] + cw_ref[2:3, :] * xl
    for j, back in ((0, 2), (1, 1), (3, -1)):
        rolled = pltpu.roll(xl, back % l, 0)
        ok = (t_idx >= back) if back > 0 else (t_idx < l + back)
        xc = xc + cw_ref[j:j + 1, :] * jnp.where(ok, rolled, 0.0)
    xcb = xc.astype(jnp.bfloat16)
    r_idx = lax.broadcasted_iota(jnp.int32, (SUBLANES, D_LRU), 0)
    nblk = l // SUBLANES
    for d in range(2):
        gr = jnp.dot(xcb, wg_ref[:, d * D_LRU:(d + 1) * D_LRU], preferred_element_type=jnp.float32)
        gr = gr + bg_ref[:, d * D_LRU:(d + 1) * D_LRU]
        gi = jnp.dot(xcb, wg_ref[:, (2 + d) * D_LRU:(3 + d) * D_LRU], preferred_element_type=jnp.float32)
        gi = gi + bg_ref[:, (2 + d) * D_LRU:(3 + d) * D_LRU]
        log_a = -LRU_C * _sigmoid(gr) * _softplus(-lam_ref[d:d + 1, :])
        a = jnp.exp(log_a)
        a_scr[d] = a
        b_scr[d] = jnp.sqrt(1.0 - a * a) * _sigmoid(gi) * xc

    def block(i, carries):
        new = []
        for d in range(2):
            blk = i if d == 0 else nblk - 1 - i
            rs = pl.ds(pl.multiple_of(blk * SUBLANES, SUBLANES), SUBLANES)
            aa = a_scr[d, rs, :]
            bb = b_scr[d, rs, :]
            for s in (1, 2, 4):
                if d == 0:
                    keep = r_idx >= s
                    sh = s
                else:
                    keep = r_idx < SUBLANES - s
                    sh = SUBLANES - s
                a_sh = jnp.where(keep, pltpu.roll(aa, sh, 0), 1.0)
                b_sh = jnp.where(keep, pltpu.roll(bb, sh, 0), 0.0)
                bb = bb + aa * b_sh
                aa = aa * a_sh
            hh = bb + aa * carries[d]
            h_scr[d, rs, :] = hh
            edge = SUBLANES - 1 if d == 0 else 0
            new.append(jnp.broadcast_to(hh[edge:edge + 1, :], (SUBLANES, D_LRU)))
        return tuple(new)

    last = lax.fori_loop(0, nblk, block,
                         tuple(jnp.broadcast_to(h0_ref[0, d:d + 1, :], (SUBLANES, D_LRU)) for d in range(2)))
    for d in range(2):
        hfin_ref[0, d:d + 1, :] = last[d][0:1, :]
    gl = xlgl_ref[0, :, D_LRU:2 * D_LRU]
    y_ref[0] = (h_scr[0] + h_scr[1]) * _silu(gl)


def _lru(xlgl, h0, wts):
    b, l, _ = xlgl.shape
    return pl.pallas_call(
        _lru_kernel, grid=(b,),
        in_specs=[
            pl.BlockSpec((1, l, 2 * D_LRU), lambda bi: (bi, 0, 0)),
            _const_spec(wts["conv_w"].shape), _const_spec(wts["conv_b"].shape),
            _const_spec(wts["wg"].shape), _const_spec(wts["bg"].shape), _const_spec(wts["lam"].shape),
            pl.BlockSpec((1, 2, D_LRU), lambda bi: (bi, 0, 0)),
        ],
        out_specs=[pl.BlockSpec((1, l, D_LRU), lambda bi: (bi, 0, 0)),
                   pl.BlockSpec((1, 2, D_LRU), lambda bi: (bi, 0, 0))],
        out_shape=[jax.ShapeDtypeStruct((b, l, D_LRU), jnp.float32),
                   jax.ShapeDtypeStruct((b, 2, D_LRU), jnp.float32)],
        scratch_shapes=[pltpu.VMEM((2, l, D_LRU), jnp.float32), pltpu.VMEM((2, l, D_LRU), jnp.float32),
                        pltpu.VMEM((2, l, D_LRU), jnp.float32)],
        compiler_params=_params(("parallel",)),
        name="rglru",
    )(xlgl, wts["conv_w"], wts["conv_b"], wts["wg"], wts["bg"], wts["lam"], h0)


def _post_kernel(alpha, x_ref, mod_ref, yf_ref, yb_ref, rwe_ref, ymla_ref, ylru_ref, wo_ref, lnx_ref, ln_ref,
                 ones_ref, o_ref):
    y = jnp.concatenate(
        [jnp.concatenate([yf_ref[0, tb, hp, 0] + yb_ref[0, tb, hp, 0] for tb in range(yf_ref.shape[1])], axis=0)
         for hp in range(D_RW // LANES)], axis=-1)
    ones = ones_ref[...]
    inv_n = 1.0 / HEAD_RW
    mu = _seg_sum(y, ones) * inv_n
    yc = y - mu
    var = _seg_sum(yc * yc, ones) * inv_n
    yn = yc * lax.rsqrt(var + GN_EPS) * lnx_ref[0:1, :] + lnx_ref[1:2, :]
    y_rw = (yn + rwe_ref[0, :, D_RW:2 * D_RW]) * _silu(rwe_ref[0, :, 0:D_RW])
    out = (_bdot(y_rw, wo_ref[0:D_RW, :]) + _bdot(ymla_ref[0], wo_ref[D_RW:D_RW + D_MLA, :])
           + _bdot(ylru_ref[0], wo_ref[D_RW + D_MLA:D_RW + D_MLA + D_LRU, :]))
    z = alpha * x_ref[0] + mod_ref[0, 2:3, :] * out
    mu_z = jnp.mean(z, axis=-1, keepdims=True)
    zc = z - mu_z
    var_z = jnp.mean(zc * zc, axis=-1, keepdims=True)
    o_ref[0] = zc * lax.rsqrt(var_z + LN_EPS) * ln_ref[0:1, :] + ln_ref[1:2, :]


def _post(x, mod, ys, rwe, ymla, ylru, wts, alpha, tile):
    b, l, d = x.shape
    nt = l // tile
    mod_b = mod.shape[0]
    tok = lambda width: pl.BlockSpec((1, tile, width), lambda bi, ti: (bi, ti, 0))
    y_fwd, y_bwd, bwd_group = (ys[0], ys[1], 0) if len(ys) == 2 else (ys[0], ys[0], 1)
    y_spec = lambda grp: pl.BlockSpec((1, tile // SCAN_STEPS, 2, 1, SCAN_STEPS, LANES),
                                      lambda bi, ti: (grp, ti, 0, bi, 0, 0))
    return pl.pallas_call(
        functools.partial(_post_kernel, alpha),
        grid=(b, nt),
        in_specs=[
            tok(d),
            pl.BlockSpec((1, 3, d), (lambda bi, ti: (bi, 0, 0)) if mod_b > 1 else (lambda bi, ti: (0, 0, 0))),
            y_spec(0), y_spec(bwd_group),
            tok(2 * D_RW), tok(D_MLA), tok(D_LRU),
            _const_spec(wts["w_out"].shape), _const_spec(wts["lnx"].shape), _const_spec(wts["ln"].shape),
            _const_spec(wts["ones"].shape),
        ],
        out_specs=tok(d),
        out_shape=jax.ShapeDtypeStruct((b, l, d), jnp.float32),
        compiler_params=_params(("parallel", "parallel")),
        name="post",
    )(x, mod, y_fwd, y_bwd, rwe, ymla, ylru, wts["w_out"], wts["lnx"], wts["ln"], wts["ones"])


def _block_diag(w):
    g, i, o = w.shape
    eye = jnp.eye(g, dtype=w.dtype)
    return jnp.einsum("gio,gh->giho", w, eye).reshape(g * i, g * o)


def _place(cols, offset, width):
    d, n = cols.shape
    return jnp.pad(cols, ((0, 0), (offset, width - offset - n)))


def _pack_layer(l, P, rope):
    bf16 = jnp.bfloat16
    w_in = P["w_in"][l]
    d = w_in.shape[0]
    o_cq = 4 * D_RW
    o_ckv = o_cq + Q_RANK
    o_kr = o_ckv + KV_RANK
    o_gm = o_kr + MLA_ROPE
    o_xl = o_gm + D_MLA
    w_kr = w_in[:, o_kr:o_kr + MLA_ROPE]
    half = MLA_ROPE // 2
    perm = np.concatenate([np.arange(half) * 2, np.arange(half) * 2 + 1])
    perm_sw = np.concatenate([np.arange(half) * 2 + 1, np.arange(half) * 2])
    if rope:
        kra = _place(w_kr[:, perm], MLA_NOPE, SLOT)
        krb = _place(w_kr[:, perm_sw], MLA_NOPE, SLOT)
    else:
        kra = _place(w_kr, 0, SLOT)
        krb = _place(w_kr, MLA_NOPE, SLOT)
    lora = jnp.concatenate([P["rw_w1"][l, 0], P["rw_w1"][l, 1], P["rw_a1"][l, 0], P["rw_a1"][l, 1]], axis=1)
    w_cat = jnp.concatenate([w_in[:, 0:o_kr], kra, krb, w_in[:, o_gm:], lora], axis=1)
    assert w_cat.shape == (d, N_CAT)

    wuq = P["mla_wuq"][l].reshape(Q_RANK, MLA_HEADS, MLA_NOPE + MLA_ROPE)
    qn, qr = wuq[..., :MLA_NOPE], wuq[..., MLA_NOPE:]
    pad_q = lambda rp: jnp.pad(jnp.concatenate([qn, rp], -1), ((0, 0), (0, 0), (0, SLOT - MLA_NOPE - MLA_ROPE)))
    if rope:
        wq_a = pad_q(qr[..., perm])
        wq_b = jnp.pad(qr[..., perm_sw], ((0, 0), (0, 0), (MLA_NOPE, SLOT - MLA_NOPE - MLA_ROPE)))
        wq = jnp.concatenate([wq_a.reshape(Q_RANK, D_SLOTS), wq_b.reshape(Q_RANK, D_SLOTS)], axis=1)
    else:
        wq = pad_q(qr).reshape(Q_RANK, D_SLOTS)
    wukv = P["mla_wukv"][l].reshape(KV_RANK, MLA_HEADS, MLA_NOPE + MLA_V)
    wk = jnp.pad(wukv[..., :MLA_NOPE], ((0, 0), (0, 0), (0, SLOT - MLA_NOPE)))
    wkv = wk.reshape(KV_RANK, D_SLOTS)
    wv_t = wukv[..., MLA_NOPE:].reshape(KV_RANK, D_MLA).T

    zeros_w2 = jnp.zeros_like(P["rw_w2"][l, 0])
    lw2 = jnp.block([[P["rw_w2"][l, 0], zeros_w2], [zeros_w2, P["rw_w2"][l, 1]]])
    la2 = jnp.block([[P["rw_a2"][l, 0], zeros_w2], [zeros_w2, P["rw_a2"][l, 1]]])
    rwp = jnp.concatenate([P["rw_w0"][l], P["rw_a0"][l], P["rw_kk"][l][None], P["rw_ka"][l][None],
                           P["rw_rk"][l].reshape(1, D_RW), jnp.zeros((1, D_RW), jnp.float32)], axis=0)
    grp = jnp.arange(D_RW) // HEAD_RW
    ones = (grp[:, None] == grp[None, :]).astype(bf16)
    wg = jnp.concatenate([_block_diag(P["lru_wa"][l, 0]), _block_diag(P["lru_wa"][l, 1]),
                          _block_diag(P["lru_wx"][l, 0]), _block_diag(P["lru_wx"][l, 1])], axis=1)
    bg = jnp.concatenate([P["lru_ba"][l, 0], P["lru_ba"][l, 1], P["lru_bx"][l, 0], P["lru_bx"][l, 1]])[None]
    return {
        "w_cat": w_cat.astype(bf16), "lw2": lw2.astype(bf16), "la2": la2.astype(bf16), "rwp": rwp,
        "wq": wq.astype(bf16), "wkv": wkv.astype(bf16), "wv_t": wv_t.astype(bf16),
        "qnorm": P["mla_qnorm"][l][None], "kvnorm": P["mla_kvnorm"][l][None], "ones": ones,
        "conv_w": P["lru_conv_w"][l], "conv_b": P["lru_conv_b"][l][None],
        "wg": wg.astype(bf16), "bg": bg, "lam": P["lru_lambda"][l],
        "w_out": P["w_out"][l].astype(bf16),
        "lnx": jnp.stack([P["rw_lnx_g"][l], P["rw_lnx_b"][l]]),
        "ln": jnp.stack([P["ln_g"][l], P["ln_b"][l]]),
        "perm": perm,
    }


def _rope_tables(l):
    rows = l // GRID_W
    row = jnp.repeat(jnp.arange(rows), GRID_W).astype(jnp.float32)
    col = jnp.tile(jnp.arange(GRID_W), rows).astype(jnp.float32)
    pairs = MLA_ROPE // 4
    inv = ROPE_BASE ** (-jnp.arange(pairs, dtype=jnp.float32) / pairs)
    ang = jnp.concatenate([row[:, None] * inv, col[:, None] * inv], axis=-1)
    cos, sin = jnp.cos(ang), jnp.sin(ang)
    tail = jnp.zeros((l, SLOT - MLA_NOPE - MLA_ROPE), jnp.float32)
    nope0 = jnp.zeros((l, MLA_NOPE), jnp.float32)
    ca = jnp.concatenate([nope0, cos, cos, tail], axis=1)
    sb = jnp.concatenate([nope0, -sin, sin, tail], axis=1)
    ta = jnp.concatenate([jnp.ones((l, MLA_NOPE), jnp.float32), cos, cos, tail], axis=1) * Q_SCALE
    tb = sb * Q_SCALE
    return ta, tb, ca, sb


def _layer(x, mod, wts, alpha, ctx, rope_tabs):
    b, l, _ = x.shape
    tile = min(l, ATTN_TILE)
    proj_tile = min(l, PROJ_TILE)
    outs = _pre(x, mod, wts, rope_tabs, proj_tile)
    rwk, rwe, q, k, v, gm, xlgl = outs[:7]
    if ctx is None:
        s0 = jnp.zeros((b, 2, N_HEAD_RW, HEAD_RW, HEAD_RW), jnp.float32)
        h0 = jnp.zeros((b, 2, D_LRU), jnp.float32)
        ctx_kv = None
    else:
        s0, ckv_ctx, kr_ctx, h0 = ctx
        kr_slot = jnp.pad(kr_ctx[..., wts["perm"]], ((0, 0), (0, 0), (MLA_NOPE, SLOT - MLA_NOPE - MLA_ROPE)))
        ctx_kv = _ctx_keyvals(ckv_ctx, kr_slot, wts["wkv"], wts["wv_t"])
    yrw, s_fin = _rwkv(rwk, s0)
    ymla = _attention(q, k, v, gm, ctx_kv, tile)
    ylru, h_fin = _lru(xlgl, h0, wts)
    x_new = _post(x, mod, yrw, rwe, ymla, ylru, wts, alpha, proj_tile)
    extras = (s_fin, outs[7], outs[8], h_fin) if ctx is None else None
    return x_new, extras


def kernel(x_prompt, x_sample, state_rwkv, cache_mla_ckv, cache_mla_krope, state_lru, c, c_ctx, w_mod, b_mod,
           w_in, rw_w0, rw_w1, rw_w2, rw_a0, rw_a1, rw_a2, rw_kk, rw_ka, rw_rk, rw_lnx_g, rw_lnx_b, mla_qnorm,
           mla_wuq, mla_kvnorm, mla_wukv, lru_conv_w, lru_conv_b, lru_wa, lru_ba, lru_wx, lru_bx, lru_lambda,
           w_out, ln_g, ln_b):
    P = dict(w_in=w_in, rw_w0=rw_w0, rw_w1=rw_w1, rw_w2=rw_w2, rw_a0=rw_a0, rw_a1=rw_a1, rw_a2=rw_a2,
             rw_kk=rw_kk, rw_ka=rw_ka, rw_rk=rw_rk, rw_lnx_g=rw_lnx_g, rw_lnx_b=rw_lnx_b, mla_qnorm=mla_qnorm,
             mla_wuq=mla_wuq, mla_kvnorm=mla_kvnorm, mla_wukv=mla_wukv, lru_conv_w=lru_conv_w,
             lru_conv_b=lru_conv_b, lru_wa=lru_wa, lru_ba=lru_ba, lru_wx=lru_wx, lru_bx=lru_bx,
             lru_lambda=lru_lambda, w_out=w_out, ln_g=ln_g, ln_b=ln_b)
    depth = w_in.shape[0]
    alpha = (2 * depth) ** 0.25
    dec_b, dec_l, d = x_sample.shape

    rows = -(-(1 + dec_b) // SUBLANES) * SUBLANES
    cvecs = jnp.concatenate([c_ctx[None, :], c, jnp.zeros((rows - 1 - dec_b, d), jnp.float32)], axis=0)
    mod_all = _modulation(cvecs, w_mod, b_mod).reshape(depth, rows, 3, d)
    rope_tabs = _rope_tables(dec_l)

    xp, xs = x_prompt, x_sample
    rw_states, ckvs, krs, lru_states = [], [], [], []
    for l in range(depth):
        xp, (s_l, ckv_l, kr_l, h_l) = _layer(xp, mod_all[l, 0:1], _pack_layer(l, P, rope=False), alpha, None, None)
        rw_states.append(s_l)
        ckvs.append(ckv_l)
        krs.append(kr_l)
        lru_states.append(h_l)
    for l in range(depth):
        ctx = (state_rwkv[:, l], cache_mla_ckv[:, l], cache_mla_krope[:, l], state_lru[:, l])
        xs, _ = _layer(xs, mod_all[l, 1:1 + dec_b], _pack_layer(l, P, rope=True), alpha, ctx, rope_tabs)
    return (xp, xs, jnp.stack(rw_states, axis=1), jnp.stack(ckvs, axis=1), jnp.stack(krs, axis=1),
            jnp.stack(lru_states, axis=1))
```

```python
import functools
import math

import jax
import jax.numpy as jnp
import numpy as np
from jax import lax
from jax.experimental import pallas as pl
from jax.experimental.pallas import tpu as pltpu

LANES = 128
SUBLANES = 8
VMEM_LIMIT_BYTES = 56 * 1024 * 1024

N_HEAD_RW = 4
HEAD_RW = 64
D_RW = N_HEAD_RW * HEAD_RW
MLA_HEADS = 8
MLA_NOPE = 64
MLA_ROPE = 32
MLA_V = 64
D_MLA = MLA_HEADS * MLA_V
Q_RANK = 256
KV_RANK = 128
D_LRU = 256
LRU_BLOCKS = 4
LRU_BLOCK = 64
GRID_W = 64
ROPE_BASE = 10000.0
SOFTMAX_SCALE = (MLA_NOPE + MLA_ROPE) ** -0.5
Q_SCALE = SOFTMAX_SCALE * math.log2(math.e)
LRU_C = 8.0
GN_EPS = 64e-5
LN_EPS = 1e-5
RMS_EPS = 1e-6
ATTN_TILE = 256
PROJ_TILE = 512
SLOT = LANES
D_SLOTS = MLA_HEADS * SLOT

C_RKVG = 0
C_CQ = C_RKVG + 4 * D_RW
C_CKV = C_CQ + Q_RANK
C_KRA = C_CKV + KV_RANK
C_KRB = C_KRA + SLOT
C_GM = C_KRB + SLOT
C_XL = C_GM + D_MLA
C_GL = C_XL + D_LRU
C_LORA = C_GL + D_LRU
N_CAT = C_LORA + 4 * 64

R_W0, R_A0, R_KK, R_KA, R_RK = 0, 2, 4, 5, 6

SCAN_STEPS = 16
SCAN_BLOCK_BYTES = 13 * 256 * 1024
SCAN_UNROLL = 8
SUM_TREE = 8
RWK_R, RWK_NKK, RWK_V, RWK_DIR = 0, 1, 2, 3
N_RWK_CHUNKS = 2 * 9
RWK_SHARED_CHUNKS = 2 * 3


def _params(sem):
    return pltpu.CompilerParams(dimension_semantics=sem, vmem_limit_bytes=VMEM_LIMIT_BYTES)


def _const_spec(shape):
    zeros = (0,) * len(shape)
    return pl.BlockSpec(shape, lambda *_: zeros)


def _split3(x):
    hi = x.astype(jnp.bfloat16)
    r1 = x - hi.astype(jnp.float32)
    mid = r1.astype(jnp.bfloat16)
    lo = (r1 - mid.astype(jnp.float32)).astype(jnp.bfloat16)
    return hi, mid, lo


def _seg_sum(x, ones_bf16):
    hi, mid, lo = _split3(x)
    dot = functools.partial(jnp.dot, preferred_element_type=jnp.float32)
    return dot(hi, ones_bf16) + dot(mid, ones_bf16) + dot(lo, ones_bf16)


def _sigmoid(x):
    return 1.0 / (1.0 + jnp.exp(-x))


def _silu(x):
    return x * _sigmoid(x)


def _softplus(x):
    return jnp.maximum(x, 0.0) + jnp.log1p(jnp.exp(-jnp.abs(x)))


def _bdot(a, b):
    return jnp.dot(a.astype(jnp.bfloat16), b, preferred_element_type=jnp.float32)


_CONTRACT_LAST = (((1,), (1,)), ((), ()))


def _values_t(wv_t, ckvn):
    return lax.dot_general(wv_t, ckvn.astype(jnp.bfloat16), _CONTRACT_LAST,
                           preferred_element_type=jnp.float32).astype(jnp.bfloat16)


def _mod_kernel(c_ref, w_ref, b_ref, o_ref):
    a = _silu(c_ref[...])
    w = w_ref[0]
    a_hi, a_mid, _ = _split3(a)
    w_hi, w_mid, _ = _split3(w)
    dot = functools.partial(jnp.dot, preferred_element_type=jnp.float32)
    o_ref[0] = dot(a_hi, w_hi) + dot(a_hi, w_mid) + dot(a_mid, w_hi) + b_ref[0]


def _modulation(cvecs, w_mod, b_mod):
    depth, d, d3 = w_mod.shape
    rows = cvecs.shape[0]
    nblk = d3 // d
    return pl.pallas_call(
        _mod_kernel,
        grid=(depth, nblk),
        in_specs=[
            pl.BlockSpec((rows, d), lambda l, j: (0, 0)),
            pl.BlockSpec((1, d, d), lambda l, j: (l, 0, j)),
            pl.BlockSpec((1, 1, d), lambda l, j: (l, 0, j)),
        ],
        out_specs=pl.BlockSpec((1, rows, d), lambda l, j: (l, 0, j)),
        out_shape=jax.ShapeDtypeStruct((depth, rows, d3), jnp.float32),
        compiler_params=_params(("parallel", "parallel")),
        name="adaln_mod",
    )(cvecs, w_mod, b_mod.reshape(depth, 1, d3))


def _pre_kernel(rope, *refs):
    if rope:
        (x_ref, mod_ref, win_ref, lw2_ref, la2_ref, rwp_ref, wq_ref, wkv_ref, wvt_ref, qn_ref, kvn_ref,
         ones_ref, ta_ref, tb_ref, ca_ref, sb_ref,
         rwk_ref, rwe_ref, q_ref, k_ref, v_ref, gm_ref, xlgl_ref) = refs
    else:
        (x_ref, mod_ref, win_ref, lw2_ref, la2_ref, rwp_ref, wq_ref, wkv_ref, wvt_ref, qn_ref, kvn_ref,
         ones_ref,
         rwk_ref, rwe_ref, q_ref, k_ref, v_ref, gm_ref, xlgl_ref, ckvn_ref, kr_ref) = refs

    x = x_ref[0]
    shift = mod_ref[0, 0:1, :]
    scale = mod_ref[0, 1:2, :]
    mu = jnp.mean(x, axis=-1, keepdims=True)
    xc = x - mu
    var = jnp.mean(xc * xc, axis=-1, keepdims=True)
    u = xc * lax.rsqrt(var + LN_EPS) * (1.0 + scale) + shift
    h = _bdot(u, win_ref[...])

    r = h[:, 0:D_RW]
    k = h[:, D_RW:2 * D_RW]
    v = h[:, 2 * D_RW:3 * D_RW]
    g = h[:, 3 * D_RW:4 * D_RW]
    lora_w = h[:, C_LORA:C_LORA + 128]
    lora_a = h[:, C_LORA + 128:C_LORA + 256]
    wl = _bdot(jnp.tanh(lora_w), lw2_ref[...])
    al = _bdot(lora_a, la2_ref[...])
    ones = ones_ref[...]
    kk = k * rwp_ref[R_KK:R_KK + 1, :]
    kk = kk / jnp.maximum(jnp.sqrt(_seg_sum(kk * kk, ones)), 1e-12)
    bonus = _seg_sum(r * k * rwp_ref[R_RK:R_RK + 1, :], ones) * v
    ka = rwp_ref[R_KA:R_KA + 1, :]
    def put(slot, val):
        for tb in range(val.shape[0] // SCAN_STEPS):
            for half in range(D_RW // LANES):
                rwk_ref[tb, 2 * slot + half, 0] = val[tb * SCAN_STEPS:(tb + 1) * SCAN_STEPS,
                                                      half * LANES:(half + 1) * LANES]

    put(RWK_R, r)
    put(RWK_NKK, -kk)
    put(RWK_V, v)
    for d in range(2):
        wl_d = wl[:, d * D_RW:(d + 1) * D_RW] + rwp_ref[R_W0 + d:R_W0 + d + 1, :]
        decay = jnp.exp(-math.exp(-0.5) * _sigmoid(wl_d))
        a_d = _sigmoid(al[:, d * D_RW:(d + 1) * D_RW] + rwp_ref[R_A0 + d:R_A0 + d + 1, :])
        put(RWK_DIR + 3 * d, decay)
        put(RWK_DIR + 3 * d + 1, kk * a_d)
        put(RWK_DIR + 3 * d + 2, k * (1.0 + (a_d - 1.0) * ka))
    rwe_ref[0, :, 0:D_RW] = g
    rwe_ref[0, :, D_RW:2 * D_RW] = bonus

    cq = h[:, C_CQ:C_CQ + Q_RANK]
    cqn = cq * lax.rsqrt(jnp.mean(cq * cq, axis=-1, keepdims=True) + RMS_EPS) * qn_ref[...]
    ckv = h[:, C_CKV:C_CKV + KV_RANK]
    ckvn = ckv * lax.rsqrt(jnp.mean(ckv * ckv, axis=-1, keepdims=True) + RMS_EPS) * kvn_ref[...]
    qq = _bdot(cqn, wq_ref[...])
    kv = _bdot(ckvn, wkv_ref[...])
    v_ref[0] = _values_t(wvt_ref[...], ckvn)
    kra = h[:, C_KRA:C_KRA + SLOT]
    krb = h[:, C_KRB:C_KRB + SLOT]
    if rope:
        k_rope = kra * ca_ref[...] + krb * sb_ref[...]
        ta = ta_ref[...]
        tb = tb_ref[...]
    else:
        k_rope = krb
        ckvn_ref[0] = ckvn
        kr_ref[0] = kra[:, 0:MLA_ROPE]
    for hd in range(MLA_HEADS):
        lo, hi = hd * SLOT, (hd + 1) * SLOT
        if rope:
            qh = qq[:, lo:hi] * ta + qq[:, D_SLOTS + lo:D_SLOTS + hi] * tb
        else:
            qh = qq[:, lo:hi] * Q_SCALE
        q_ref[0, :, lo:hi] = qh.astype(jnp.bfloat16)
        k_ref[0, :, lo:hi] = (kv[:, lo:hi] + k_rope).astype(jnp.bfloat16)
    gm_ref[0] = h[:, C_GM:C_GM + D_MLA]
    xlgl_ref[0] = h[:, C_XL:C_XL + 2 * D_LRU]


def _pre(x, mod, wts, rope_tabs, tile):
    b, l, d = x.shape
    rope = rope_tabs is not None
    nt = l // tile
    mod_b = mod.shape[0]
    tok = lambda width: pl.BlockSpec((1, tile, width), lambda bi, ti: (bi, ti, 0))
    in_specs = [
        tok(d),
        pl.BlockSpec((1, 3, d), (lambda bi, ti: (bi, 0, 0)) if mod_b > 1 else (lambda bi, ti: (0, 0, 0))),
        _const_spec(wts["w_cat"].shape), _const_spec(wts["lw2"].shape), _const_spec(wts["la2"].shape),
        _const_spec(wts["rwp"].shape), _const_spec(wts["wq"].shape), _const_spec(wts["wkv"].shape),
        _const_spec(wts["wv_t"].shape), _const_spec(wts["qnorm"].shape), _const_spec(wts["kvnorm"].shape), _const_spec(wts["ones"].shape),
    ]
    args = [x, mod, wts["w_cat"], wts["lw2"], wts["la2"], wts["rwp"], wts["wq"], wts["wkv"], wts["wv_t"],
            wts["qnorm"], wts["kvnorm"], wts["ones"]]
    f32, bf16 = jnp.float32, jnp.bfloat16
    out_shape = [
        jax.ShapeDtypeStruct((l // SCAN_STEPS, N_RWK_CHUNKS, b, SCAN_STEPS, LANES), f32),
        jax.ShapeDtypeStruct((b, l, 2 * D_RW), f32),
        jax.ShapeDtypeStruct((b, l, D_SLOTS), bf16), jax.ShapeDtypeStruct((b, l, D_SLOTS), bf16),
        jax.ShapeDtypeStruct((b, D_MLA, l), bf16), jax.ShapeDtypeStruct((b, l, D_MLA), f32),
        jax.ShapeDtypeStruct((b, l, 2 * D_LRU), f32),
    ]
    rwk_spec = pl.BlockSpec((tile // SCAN_STEPS, N_RWK_CHUNKS, 1, SCAN_STEPS, LANES),
                            lambda bi, ti: (ti, 0, bi, 0, 0))
    vt_spec = pl.BlockSpec((1, D_MLA, tile), lambda bi, ti: (bi, 0, ti))
    out_specs = [rwk_spec, tok(2 * D_RW), tok(D_SLOTS), tok(D_SLOTS), vt_spec, tok(D_MLA), tok(2 * D_LRU)]
    if rope:
        tab = pl.BlockSpec((tile, SLOT), lambda bi, ti: (ti, 0))
        in_specs += [tab, tab, tab, tab]
        args += list(rope_tabs)
    else:
        out_shape += [jax.ShapeDtypeStruct((b, l, KV_RANK), f32), jax.ShapeDtypeStruct((b, l, MLA_ROPE), f32)]
        out_specs += [tok(KV_RANK), tok(MLA_ROPE)]
    return pl.pallas_call(
        functools.partial(_pre_kernel, rope),
        grid=(b, nt), in_specs=in_specs, out_specs=out_specs, out_shape=out_shape,
        compiler_params=_params(("parallel", "parallel")),
        name="pre_rope" if rope else "pre_ctx",
    )(*args)


KT_R, KT_W, KT_B, KT_KD = range(4)
N_KT = 4
PLANES = 2


def _scan_kernel(dl, vs, steps, *refs):
    sh_refs, dr_refs = refs[0:dl], refs[dl:2 * dl]
    s0_ref = refs[2 * dl]
    y_refs = refs[2 * dl + 1:3 * dl + 1]
    sfin_ref = refs[3 * dl + 1]
    s_scr = refs[3 * dl + 2]
    kt_scrs, vv_scrs, nkk_scrs = refs[3 * dl + 3:3 * dl + 5], refs[3 * dl + 5:3 * dl + 7], refs[3 * dl + 7:]
    g = pl.program_id(0)
    i = pl.program_id(1)
    nsub = sh_refs[0].shape[0]
    rows_per_chunk = sh_refs[0].shape[1] // RWK_SHARED_CHUNKS
    sub_steps = SCAN_STEPS
    batch = rows_per_chunk // sub_steps
    vr = HEAD_RW // vs
    lane_split = (lax.broadcasted_iota(jnp.int32, (vr, LANES), 1) // (2 * batch)) % vs

    def block_row(e, j):
        if dl == 2:
            jj = j if e == 0 else steps - 1 - j
        else:
            jj = j + g * (steps - 1 - 2 * j)
        return jj // sub_steps, jj % sub_steps

    def batch_rows(ref, chunk, pos):
        sub, t = pos
        return ref[sub, pl.ds(chunk * rows_per_chunk + t, batch, stride=sub_steps), :]

    def gather(in_refs, slot, j):
        parts = []
        for e in range(dl):
            t = block_row(e, j)
            pair = [batch_rows(in_refs[e], 2 * slot + hp, t) for hp in range(2)]
            parts += pair * vs
        return jnp.concatenate(parts, axis=0)

    @pl.when(i == 0)
    def _():
        s_scr[...] = s0_ref[0]

    def prepare_tile(kt, tile, in_refs, slot, j):
        kt[tile] = gather(in_refs, slot, j).T

    def prepare_v(j, buf):
        v_t = gather(sh_refs, RWK_V, j).T
        for p in range(PLANES):
            base = p * HEAD_RW
            vv = v_t[base:base + vr, :]
            for s in range(1, vs):
                vv = jnp.where(lane_split == s, v_t[base + s * vr:base + (s + 1) * vr, :], vv)
            vv_scrs[buf][p] = vv

    def prepare_nkk(j, buf):
        nkk_scrs[buf][...] = gather(sh_refs, RWK_NKK, jnp.minimum(j, steps - 1)).T

    def row(ref, *idx):
        return jnp.broadcast_to(ref[idx[:-1] + (pl.ds(idx[-1], 1), slice(None))], (vr, LANES))

    def tree_sum(terms):
        while len(terms) > 1:
            terms = [terms[a] + terms[a + 1] for a in range(0, len(terms), 2)]
        return terms[0]

    def emit(j, y_planes):
        zero = jnp.zeros((vr, LANES), jnp.float32)
        y_full = jnp.concatenate([jnp.where(lane_split == s, y_planes[p], zero)
                                  for p in range(PLANES) for s in range(vs)], axis=0)
        y_t = y_full.T
        for e in range(dl):
            sub, t = block_row(e, j)
            for hp in range(2):
                parts = [y_t[((e * vs + s) * 2 + hp) * batch:((e * vs + s) * 2 + hp + 1) * batch, :]
                         for s in range(vs)]
                y_refs[e][0, sub, pl.ds(hp * rows_per_chunk + t, batch, stride=sub_steps), :] = tree_sum(parts)

    for tile, in_refs, slot in ((KT_R, sh_refs, RWK_R), (KT_W, dr_refs, 0), (KT_B, dr_refs, 1), (KT_KD, dr_refs, 2)):
        prepare_tile(kt_scrs[0], tile, in_refs, slot, 0)
    prepare_v(0, 0)
    prepare_nkk(0, 0)
    prepare_nkk(1, 1)
    sa_first = tuple(
        tree_sum([tree_sum([s_scr[p, k] * row(nkk_scrs[0], p * HEAD_RW + k) for k in range(k0, HEAD_RW, 4)])
                  for k0 in range(4)])
        for p in range(PLANES))

    def step(j, cur, sa, y_prev):
        nxt = 1 - cur
        kt, vv_ref, nkk_next = kt_scrs[cur], vv_scrs[cur], nkk_scrs[nxt]
        j1 = jnp.minimum(j + 1, steps - 1)
        side_work = [
            lambda: emit(jnp.maximum(j - 1, 0), y_prev),
            lambda: prepare_tile(kt_scrs[nxt], KT_W, dr_refs, 0, j1),
            lambda: prepare_tile(kt_scrs[nxt], KT_B, dr_refs, 1, j1),
            lambda: prepare_tile(kt_scrs[nxt], KT_KD, dr_refs, 2, j1),
            lambda: prepare_tile(kt_scrs[nxt], KT_R, sh_refs, RWK_R, j1),
            lambda: prepare_v(j1, nxt),
            lambda: prepare_nkk(j + 2, cur),
        ]
        chunk = PLANES * HEAD_RW // (2 * (len(side_work) + 1))
        y_planes, sa_next = [], []
        done = 0
        for p in range(PLANES):
            base = p * HEAD_RW
            vv = vv_ref[p]
            y_acc, sa_acc, y_terms, sa_terms = None, None, [], []
            for k in range(HEAD_RW):
                s_new = s_scr[p, k] * row(kt, KT_W, base + k) + (
                    sa[p] * row(kt, KT_B, base + k) + vv * row(kt, KT_KD, base + k))
                s_scr[p, k] = s_new
                y_terms.append(s_new * row(kt, KT_R, base + k))
                sa_terms.append(s_new * row(nkk_next, base + k))
                if len(y_terms) == SUM_TREE:
                    y_sum, sa_sum = tree_sum(y_terms), tree_sum(sa_terms)
                    y_acc = y_sum if y_acc is None else y_acc + y_sum
                    sa_acc = sa_sum if sa_acc is None else sa_acc + sa_sum
                    y_terms, sa_terms = [], []
                done += 1
                if done % chunk == 0 and side_work:
                    side_work.pop(0)()
            y_planes.append(y_acc)
            sa_next.append(sa_acc)
        return tuple(sa_next), tuple(y_planes)

    def unrolled_steps(jj, carry):
        for u in range(SCAN_UNROLL):
            carry = step(SCAN_UNROLL * jj + u, u % 2, *carry)
        return carry

    zeros = tuple(jnp.zeros((vr, LANES), jnp.float32) for _ in range(PLANES))
    _, y_last = lax.fori_loop(0, steps // SCAN_UNROLL, unrolled_steps, (sa_first, zeros))
    emit(steps - 1, y_last)

    @pl.when(i == pl.num_programs(1) - 1)
    def _():
        sfin_ref[0] = s_scr[...]


def _lane_plan(batch):
    per_dir = (N_HEAD_RW // PLANES) * batch
    assert LANES % per_dir == 0
    fill = LANES // per_dir
    dl = 2 if fill >= 4 else 1
    vs = fill // dl
    assert dl * vs * per_dir == LANES and HEAD_RW % (vs * SUBLANES) == 0
    return 2 // dl, dl, vs


def _rwkv_scan(rwk, s0_lanes, plan):
    groups, dl, vs = plan
    nt, chunks, b, steps, _ = rwk.shape
    vr = HEAD_RW // vs
    rows = b * steps
    rwk_rows = rwk.reshape(nt, chunks * rows, LANES)
    nsub = max(1, min(nt, SCAN_BLOCK_BYTES // (RWK_SHARED_CHUNKS * rows * LANES * 4)))
    while nt % nsub:
        nsub -= 1
    ng = nt // nsub

    def time_block(d, ti):
        return ti + d * (ng - 1 - 2 * ti)

    def slot_dir(e, gi):
        return e if dl == 2 else gi

    blk = (nsub, RWK_SHARED_CHUNKS * rows, LANES)
    in_specs = [pl.BlockSpec(blk, functools.partial(lambda e, gi, ti: (time_block(slot_dir(e, gi), ti), 0, 0), e))
                for e in range(dl)]
    in_specs += [pl.BlockSpec(blk, functools.partial(
        lambda e, gi, ti: (time_block(slot_dir(e, gi), ti), 1 + slot_dir(e, gi), 0), e)) for e in range(dl)]
    state_spec = pl.BlockSpec((1, PLANES, HEAD_RW, vr, LANES), lambda gi, ti: (gi, 0, 0, 0, 0))
    in_specs.append(state_spec)
    y_specs = [pl.BlockSpec((1, nsub, 2 * rows, LANES), functools.partial(
        lambda e, gi, ti: (gi, time_block(slot_dir(e, gi), ti), 0, 0), e)) for e in range(dl)]
    outs = pl.pallas_call(
        functools.partial(_scan_kernel, dl, vs, nsub * steps),
        grid=(groups, ng),
        in_specs=in_specs,
        out_specs=y_specs + [state_spec],
        out_shape=[jax.ShapeDtypeStruct((groups, nt, 2 * rows, LANES), jnp.float32)] * dl
        + [jax.ShapeDtypeStruct((groups, PLANES, HEAD_RW, vr, LANES), jnp.float32)],
        scratch_shapes=[pltpu.VMEM((PLANES, HEAD_RW, vr, LANES), jnp.float32),
                        pltpu.VMEM((N_KT, LANES, LANES), jnp.float32), pltpu.VMEM((N_KT, LANES, LANES), jnp.float32),
                        pltpu.VMEM((PLANES, vr, LANES), jnp.float32), pltpu.VMEM((PLANES, vr, LANES), jnp.float32),
                        pltpu.VMEM((LANES, LANES), jnp.float32), pltpu.VMEM((LANES, LANES), jnp.float32)],
        compiler_params=_params(("parallel", "arbitrary")),
        name="rwkv_scan",
    )(*([rwk_rows] * (2 * dl)), s0_lanes)
    ys = [y.reshape(groups, nt, 2, b, steps, LANES) for y in outs[:dl]]
    return ys, outs[dl]


def _state_to_lanes(s, plan):
    g, dl, vs = plan
    b = s.shape[0]
    vr = HEAD_RW // vs
    t = s.reshape(b, g, dl, N_HEAD_RW // PLANES, PLANES, vs, vr, HEAD_RW)
    t = jnp.transpose(t, (1, 4, 7, 6, 2, 5, 3, 0))
    return t.reshape(g, PLANES, HEAD_RW, vr, LANES)


def _state_from_lanes(s, plan, b):
    g, dl, vs = plan
    vr = HEAD_RW // vs
    t = s.reshape(g, PLANES, HEAD_RW, vr, dl, vs, N_HEAD_RW // PLANES, b)
    t = jnp.transpose(t, (7, 0, 4, 6, 1, 5, 3, 2))
    return t.reshape(b, 2, N_HEAD_RW, HEAD_RW, HEAD_RW)


def _rwkv(rwk, s0):
    b = rwk.shape[2]
    plan = _lane_plan(b)
    ys, s_fin = _rwkv_scan(rwk, _state_to_lanes(s0, plan), plan)
    return ys, _state_from_lanes(s_fin, plan, b)


LRU_UNROLL = 4
ROW_GROUP = 64


def _reduce_rows(x, op):
    r, c = x.shape
    if r > ROW_GROUP and r % ROW_GROUP == 0:
        x = op(x.reshape(r // ROW_GROUP, ROW_GROUP, c), axis=0)
    return op(x, axis=0, keepdims=True)


def _attn_kernel(has_ctx, *refs):
    if has_ctx:
        q_ref, k_ref, v_ref, kc_ref, vc_ref, gm_ref, o_ref = refs
    else:
        q_ref, k_ref, v_ref, gm_ref, o_ref = refs
    def scores(hd):
        sl = slice(hd * SLOT, (hd + 1) * SLOT)
        q = q_ref[0, :, sl]
        s = lax.dot_general(k_ref[0, :, sl], q, _CONTRACT_LAST, preferred_element_type=jnp.float32)
        sc = (lax.dot_general(kc_ref[0, :, sl], q, _CONTRACT_LAST, preferred_element_type=jnp.float32)
              if has_ctx else None)
        return s, sc

    def col_max(s, sc):
        m = _reduce_rows(s, jnp.max)
        return jnp.maximum(m, _reduce_rows(sc, jnp.max)) if has_ctx else m

    ready = [scores(0), scores(1)]
    m_next = col_max(*ready[0])
    outs = []
    for hd in range(MLA_HEADS):
        s, sc = ready.pop(0)
        m = m_next
        if hd + 2 < MLA_HEADS:
            ready.append(scores(hd + 2))
        if hd + 1 < MLA_HEADS:
            m_next = col_max(*ready[0])
        vrows = slice(hd * MLA_V, (hd + 1) * MLA_V)
        p = jnp.exp2(s - m)
        den = _reduce_rows(p, jnp.sum)
        o = jnp.dot(v_ref[0, vrows, :], p.astype(jnp.bfloat16), preferred_element_type=jnp.float32)
        if has_ctx:
            pc = jnp.exp2(sc - m)
            den = den + _reduce_rows(pc, jnp.sum)
            o = o + jnp.dot(vc_ref[0, vrows, :], pc.astype(jnp.bfloat16), preferred_element_type=jnp.float32)
        outs.append(o * (1.0 / den))
        if hd % 2 == 1:
            cols = slice((hd // 2) * LANES, (hd // 2 + 1) * LANES)
            o_ref[0, :, cols] = jnp.concatenate(outs, axis=0).T * _silu(gm_ref[0, :, cols])
            outs = []


def _attention(q, k, v, gm, ctx_kv, tile):
    b, l, _ = q.shape
    nt = l // tile
    has_ctx = ctx_kv is not None
    tok = lambda width: pl.BlockSpec((1, tile, width), lambda bi, ti: (bi, ti, 0))
    full = lambda arr: pl.BlockSpec((1,) + arr.shape[1:], lambda bi, ti: (bi, 0, 0))
    in_specs = [tok(D_SLOTS), full(k), full(v)]
    args = [q, k, v]
    if has_ctx:
        in_specs += [full(ctx_kv[0]), full(ctx_kv[1])]
        args += list(ctx_kv)
    in_specs.append(tok(D_MLA))
    args.append(gm)
    return pl.pallas_call(
        functools.partial(_attn_kernel, has_ctx),
        grid=(b, nt), in_specs=in_specs, out_specs=tok(D_MLA),
        out_shape=jax.ShapeDtypeStruct((b, l, D_MLA), jnp.float32),
        compiler_params=_params(("parallel", "parallel")),
        name="mla_attn_ctx" if has_ctx else "mla_attn",
    )(*args)


def _ctxkv_kernel(ckv_ref, kr_ref, wkv_ref, wvt_ref, k_ref, v_ref):
    kv = _bdot(ckv_ref[0], wkv_ref[...])
    kr = kr_ref[0]
    for hd in range(MLA_HEADS):
        sl = slice(hd * SLOT, (hd + 1) * SLOT)
        k_ref[0, :, sl] = (kv[:, sl] + kr).astype(jnp.bfloat16)
    v_ref[0] = _values_t(wvt_ref[...], ckv_ref[0])


def _ctx_keyvals(ckv_ctx, kr_slot, wkv, wv_t):
    b, p, _ = ckv_ctx.shape
    blk = lambda width: pl.BlockSpec((1, p, width), lambda bi: (bi, 0, 0))
    return pl.pallas_call(
        _ctxkv_kernel, grid=(b,),
        in_specs=[blk(KV_RANK), blk(SLOT), _const_spec(wkv.shape), _const_spec(wv_t.shape)],
        out_specs=[blk(D_SLOTS), pl.BlockSpec((1, D_MLA, p), lambda bi: (bi, 0, 0))],
        out_shape=[jax.ShapeDtypeStruct((b, p, D_SLOTS), jnp.bfloat16),
                   jax.ShapeDtypeStruct((b, D_MLA, p), jnp.bfloat16)],
        compiler_params=_params(("parallel",)),
        name="ctx_keyvals",
    )(ckv_ctx, kr_slot, wkv, wv_t)


def _lru_kernel(xlgl_ref, cw_ref, cb_ref, wg_ref, bg_ref, lam_ref, h0_ref, y_ref, hfin_ref,
                a_scr, b_scr, h_scr):
    l = xlgl_ref.shape[1]
    xl = xlgl_ref[0, :, 0:D_LRU]
    t_idx = lax.broadcasted_iota(jnp.int32, (l, D_LRU), 0)
    xc = cb_ref[...] + cw_ref[2:3, :] * xl
    for j, back in ((0, 2), (1, 1), (3, -1)):
        rolled = pltpu.roll(xl, back % l, 0)
        ok = (t_idx >= back) if back > 0 else (t_idx < l + back)
        xc = xc + cw_ref[j:j + 1, :] * jnp.where(ok, rolled, 0.0)
    xcb = xc.astype(jnp.bfloat16)
    r_idx = lax.broadcasted_iota(jnp.int32, (SUBLANES, D_LRU), 0)
    nblk = l // SUBLANES
    for d in range(2):
        gr = jnp.dot(xcb, wg_ref[:, d * D_LRU:(d + 1) * D_LRU], preferred_element_type=jnp.float32)
        gr = gr + bg_ref[:, d * D_LRU:(d + 1) * D_LRU]
        gi = jnp.dot(xcb, wg_ref[:, (2 + d) * D_LRU:(3 + d) * D_LRU], preferred_element_type=jnp.float32)
        gi = gi + bg_ref[:, (2 + d) * D_LRU:(3 + d) * D_LRU]
        log_a = -LRU_C * _sigmoid(gr) * _softplus(-lam_ref[d:d + 1, :])
        a = jnp.exp(log_a)
        a_scr[d] = a
        b_scr[d] = jnp.sqrt(1.0 - a * a) * _sigmoid(gi) * xc

    def block(i, carries):
        new = []
        for d in range(2):
            blk = i if d == 0 else nblk - 1 - i
            rs = pl.ds(pl.multiple_of(blk * SUBLANES, SUBLANES), SUBLANES)
            aa = a_scr[d, rs, :]
            bb = b_scr[d, rs, :]
            for s in (1, 2, 4):
                if d == 0:
                    keep = r_idx >= s
                    sh = s
                else:
                    keep = r_idx < SUBLANES - s
                    sh = SUBLANES - s
                a_sh = jnp.where(keep, pltpu.roll(aa, sh, 0), 1.0)
                b_sh = jnp.where(keep, pltpu.roll(bb, sh, 0), 0.0)
                bb = bb + aa * b_sh
                aa = aa * a_sh
            hh = bb + aa * carries[d]
            h_scr[d, rs, :] = hh
            edge = SUBLANES - 1 if d == 0 else 0
            new.append(jnp.broadcast_to(hh[edge:edge + 1, :], (SUBLANES, D_LRU)))
        return tuple(new)

    def blocks(ii, carries):
        for u in range(LRU_UNROLL):
            carries = block(ii * LRU_UNROLL + u, carries)
        return carries

    last = lax.fori_loop(0, nblk // LRU_UNROLL, blocks,
                         tuple(jnp.broadcast_to(h0_ref[0, d:d + 1, :], (SUBLANES, D_LRU)) for d in range(2)))
    for d in range(2):
        hfin_ref[0, d:d + 1, :] = last[d][0:1, :]
    gl = xlgl_ref[0, :, D_LRU:2 * D_LRU]
    y_ref[0] = (h_scr[0] + h_scr[1]) * _silu(gl)


def _lru(xlgl, h0, wts):
    b, l, _ = xlgl.shape
    return pl.pallas_call(
        _lru_kernel, grid=(b,),
        in_specs=[
            pl.BlockSpec((1, l, 2 * D_LRU), lambda bi: (bi, 0, 0)),
            _const_spec(wts["conv_w"].shape), _const_spec(wts["conv_b"].shape),
            _const_spec(wts["wg"].shape), _const_spec(wts["bg"].shape), _const_spec(wts["lam"].shape),
            pl.BlockSpec((1, 2, D_LRU), lambda bi: (bi, 0, 0)),
        ],
        out_specs=[pl.BlockSpec((1, l, D_LRU), lambda bi: (bi, 0, 0)),
                   pl.BlockSpec((1, 2, D_LRU), lambda bi: (bi, 0, 0))],
        out_shape=[jax.ShapeDtypeStruct((b, l, D_LRU), jnp.float32),
                   jax.ShapeDtypeStruct((b, 2, D_LRU), jnp.float32)],
        scratch_shapes=[pltpu.VMEM((2, l, D_LRU), jnp.float32), pltpu.VMEM((2, l, D_LRU), jnp.float32),
                        pltpu.VMEM((2, l, D_LRU), jnp.float32)],
        compiler_params=_params(("parallel",)),
        name="rglru",
    )(xlgl, wts["conv_w"], wts["conv_b"], wts["wg"], wts["bg"], wts["lam"], h0)


def _post_kernel(alpha, x_ref, mod_ref, yf_ref, yb_ref, rwe_ref, ymla_ref, ylru_ref, wo_ref, lnx_ref, ln_ref,
                 ones_ref, o_ref):
    y = jnp.concatenate(
        [jnp.concatenate([yf_ref[0, tb, hp, 0] + yb_ref[0, tb, hp, 0] for tb in range(yf_ref.shape[1])], axis=0)
         for hp in range(D_RW // LANES)], axis=-1)
    ones = ones_ref[...]
    inv_n = 1.0 / HEAD_RW
    mu = _seg_sum(y, ones) * inv_n
    yc = y - mu
    var = _seg_sum(yc * yc, ones) * inv_n
    yn = yc * lax.rsqrt(var + GN_EPS) * lnx_ref[0:1, :] + lnx_ref[1:2, :]
    y_rw = (yn + rwe_ref[0, :, D_RW:2 * D_RW]) * _silu(rwe_ref[0, :, 0:D_RW])
    out = (_bdot(y_rw, wo_ref[0:D_RW, :]) + _bdot(ymla_ref[0], wo_ref[D_RW:D_RW + D_MLA, :])
           + _bdot(ylru_ref[0], wo_ref[D_RW + D_MLA:D_RW + D_MLA + D_LRU, :]))
    z = alpha * x_ref[0] + mod_ref[0, 2:3, :] * out
    mu_z = jnp.mean(z, axis=-1, keepdims=True)
    zc = z - mu_z
    var_z = jnp.mean(zc * zc, axis=-1, keepdims=True)
    o_ref[0] = zc * lax.rsqrt(var_z + LN_EPS) * ln_ref[0:1, :] + ln_ref[1:2, :]


def _post(x, mod, ys, rwe, ymla, ylru, wts, alpha, tile):
    b, l, d = x.shape
    nt = l // tile
    mod_b = mod.shape[0]
    tok = lambda width: pl.BlockSpec((1, tile, width), lambda bi, ti: (bi, ti, 0))
    y_fwd, y_bwd, bwd_group = (ys[0], ys[1], 0) if len(ys) == 2 else (ys[0], ys[0], 1)
    y_spec = lambda grp: pl.BlockSpec((1, tile // SCAN_STEPS, 2, 1, SCAN_STEPS, LANES),
                                      lambda bi, ti: (grp, ti, 0, bi, 0, 0))
    return pl.pallas_call(
        functools.partial(_post_kernel, alpha),
        grid=(b, nt),
        in_specs=[
            tok(d),
            pl.BlockSpec((1, 3, d), (lambda bi, ti: (bi, 0, 0)) if mod_b > 1 else (lambda bi, ti: (0, 0, 0))),
            y_spec(0), y_spec(bwd_group),
            tok(2 * D_RW), tok(D_MLA), tok(D_LRU),
            _const_spec(wts["w_out"].shape), _const_spec(wts["lnx"].shape), _const_spec(wts["ln"].shape),
            _const_spec(wts["ones"].shape),
        ],
        out_specs=tok(d),
        out_shape=jax.ShapeDtypeStruct((b, l, d), jnp.float32),
        compiler_params=_params(("parallel", "parallel")),
        name="post",
    )(x, mod, y_fwd, y_bwd, rwe, ymla, ylru, wts["w_out"], wts["lnx"], wts["ln"], wts["ones"])


def _block_diag(w):
    g, i, o = w.shape
    eye = jnp.eye(g, dtype=w.dtype)
    return jnp.einsum("gio,gh->giho", w, eye).reshape(g * i, g * o)


def _place(cols, offset, width):
    d, n = cols.shape
    return jnp.pad(cols, ((0, 0), (offset, width - offset - n)))


def _pack_layer(l, P, rope):
    bf16 = jnp.bfloat16
    w_in = P["w_in"][l]
    d = w_in.shape[0]
    o_cq = 4 * D_RW
    o_ckv = o_cq + Q_RANK
    o_kr = o_ckv + KV_RANK
    o_gm = o_kr + MLA_ROPE
    o_xl = o_gm + D_MLA
    w_kr = w_in[:, o_kr:o_kr + MLA_ROPE]
    half = MLA_ROPE // 2
    perm = np.concatenate([np.arange(half) * 2, np.arange(half) * 2 + 1])
    perm_sw = np.concatenate([np.arange(half) * 2 + 1, np.arange(half) * 2])
    if rope:
        kra = _place(w_kr[:, perm], MLA_NOPE, SLOT)
        krb = _place(w_kr[:, perm_sw], MLA_NOPE, SLOT)
    else:
        kra = _place(w_kr, 0, SLOT)
        krb = _place(w_kr, MLA_NOPE, SLOT)
    lora = jnp.concatenate([P["rw_w1"][l, 0], P["rw_w1"][l, 1], P["rw_a1"][l, 0], P["rw_a1"][l, 1]], axis=1)
    w_cat = jnp.concatenate([w_in[:, 0:o_kr], kra, krb, w_in[:, o_gm:], lora], axis=1)
    assert w_cat.shape == (d, N_CAT)

    wuq = P["mla_wuq"][l].reshape(Q_RANK, MLA_HEADS, MLA_NOPE + MLA_ROPE)
    qn, qr = wuq[..., :MLA_NOPE], wuq[..., MLA_NOPE:]
    pad_q = lambda rp: jnp.pad(jnp.concatenate([qn, rp], -1), ((0, 0), (0, 0), (0, SLOT - MLA_NOPE - MLA_ROPE)))
    if rope:
        wq_a = pad_q(qr[..., perm])
        wq_b = jnp.pad(qr[..., perm_sw], ((0, 0), (0, 0), (MLA_NOPE, SLOT - MLA_NOPE - MLA_ROPE)))
        wq = jnp.concatenate([wq_a.reshape(Q_RANK, D_SLOTS), wq_b.reshape(Q_RANK, D_SLOTS)], axis=1)
    else:
        wq = pad_q(qr).reshape(Q_RANK, D_SLOTS)
    wukv = P["mla_wukv"][l].reshape(KV_RANK, MLA_HEADS, MLA_NOPE + MLA_V)
    wk = jnp.pad(wukv[..., :MLA_NOPE], ((0, 0), (0, 0), (0, SLOT - MLA_NOPE)))
    wkv = wk.reshape(KV_RANK, D_SLOTS)
    wv_t = wukv[..., MLA_NOPE:].reshape(KV_RANK, D_MLA).T

    zeros_w2 = jnp.zeros_like(P["rw_w2"][l, 0])
    lw2 = jnp.block([[P["rw_w2"][l, 0], zeros_w2], [zeros_w2, P["rw_w2"][l, 1]]])
    la2 = jnp.block([[P["rw_a2"][l, 0], zeros_w2], [zeros_w2, P["rw_a2"][l, 1]]])
    rwp = jnp.concatenate([P["rw_w0"][l], P["rw_a0"][l], P["rw_kk"][l][None], P["rw_ka"][l][None],
                           P["rw_rk"][l].reshape(1, D_RW), jnp.zeros((1, D_RW), jnp.float32)], axis=0)
    grp = jnp.arange(D_RW) // HEAD_RW
    ones = (grp[:, None] == grp[None, :]).astype(bf16)
    wg = jnp.concatenate([_block_diag(P["lru_wa"][l, 0]), _block_diag(P["lru_wa"][l, 1]),
                          _block_diag(P["lru_wx"][l, 0]), _block_diag(P["lru_wx"][l, 1])], axis=1)
    bg = jnp.concatenate([P["lru_ba"][l, 0], P["lru_ba"][l, 1], P["lru_bx"][l, 0], P["lru_bx"][l, 1]])[None]
    return {
        "w_cat": w_cat.astype(bf16), "lw2": lw2.astype(bf16), "la2": la2.astype(bf16), "rwp": rwp,
        "wq": wq.astype(bf16), "wkv": wkv.astype(bf16), "wv_t": wv_t.astype(bf16),
        "qnorm": P["mla_qnorm"][l][None], "kvnorm": P["mla_kvnorm"][l][None], "ones": ones,
        "conv_w": P["lru_conv_w"][l], "conv_b": P["lru_conv_b"][l][None],
        "wg": wg.astype(bf16), "bg": bg, "lam": P["lru_lambda"][l],
        "w_out": P["w_out"][l].astype(bf16),
        "lnx": jnp.stack([P["rw_lnx_g"][l], P["rw_lnx_b"][l]]),
        "ln": jnp.stack([P["ln_g"][l], P["ln_b"][l]]),
        "perm": perm,
    }


def _rope_tables(l):
    rows = l // GRID_W
    row = jnp.repeat(jnp.arange(rows), GRID_W).astype(jnp.float32)
    col = jnp.tile(jnp.arange(GRID_W), rows).astype(jnp.float32)
    pairs = MLA_ROPE // 4
    inv = ROPE_BASE ** (-jnp.arange(pairs, dtype=jnp.float32) / pairs)
    ang = jnp.concatenate([row[:, None] * inv, col[:, None] * inv], axis=-1)
    cos, sin = jnp.cos(ang), jnp.sin(ang)
    tail = jnp.zeros((l, SLOT - MLA_NOPE - MLA_ROPE), jnp.float32)
    nope0 = jnp.zeros((l, MLA_NOPE), jnp.float32)
    ca = jnp.concatenate([nope0, cos, cos, tail], axis=1)
    sb = jnp.concatenate([nope0, -sin, sin, tail], axis=1)
    ta = jnp.concatenate([jnp.ones((l, MLA_NOPE), jnp.float32), cos, cos, tail], axis=1) * Q_SCALE
    tb = sb * Q_SCALE
    return ta, tb, ca, sb


def _layer(x, mod, wts, alpha, ctx, rope_tabs):
    b, l, _ = x.shape
    tile = min(l, ATTN_TILE)
    proj_tile = min(l, PROJ_TILE)
    outs = _pre(x, mod, wts, rope_tabs, proj_tile)
    rwk, rwe, q, k, v, gm, xlgl = outs[:7]
    if ctx is None:
        s0 = jnp.zeros((b, 2, N_HEAD_RW, HEAD_RW, HEAD_RW), jnp.float32)
        h0 = jnp.zeros((b, 2, D_LRU), jnp.float32)
        ctx_kv = None
    else:
        s0, ckv_ctx, kr_ctx, h0 = ctx
        kr_slot = jnp.pad(kr_ctx[..., wts["perm"]], ((0, 0), (0, 0), (MLA_NOPE, SLOT - MLA_NOPE - MLA_ROPE)))
        ctx_kv = _ctx_keyvals(ckv_ctx, kr_slot, wts["wkv"], wts["wv_t"])
    yrw, s_fin = _rwkv(rwk, s0)
    ymla = _attention(q, k, v, gm, ctx_kv, tile)
    ylru, h_fin = _lru(xlgl, h0, wts)
    x_new = _post(x, mod, yrw, rwe, ymla, ylru, wts, alpha, proj_tile)
    extras = (s_fin, outs[7], outs[8], h_fin) if ctx is None else None
    return x_new, extras


def kernel(x_prompt, x_sample, state_rwkv, cache_mla_ckv, cache_mla_krope, state_lru, c, c_ctx, w_mod, b_mod,
           w_in, rw_w0, rw_w1, rw_w2, rw_a0, rw_a1, rw_a2, rw_kk, rw_ka, rw_rk, rw_lnx_g, rw_lnx_b, mla_qnorm,
           mla_wuq, mla_kvnorm, mla_wukv, lru_conv_w, lru_conv_b, lru_wa, lru_ba, lru_wx, lru_bx, lru_lambda,
           w_out, ln_g, ln_b):
    P = dict(w_in=w_in, rw_w0=rw_w0, rw_w1=rw_w1, rw_w2=rw_w2, rw_a0=rw_a0, rw_a1=rw_a1, rw_a2=rw_a2,
             rw_kk=rw_kk, rw_ka=rw_ka, rw_rk=rw_rk, rw_lnx_g=rw_lnx_g, rw_lnx_b=rw_lnx_b, mla_qnorm=mla_qnorm,
             mla_wuq=mla_wuq, mla_kvnorm=mla_kvnorm, mla_wukv=mla_wukv, lru_conv_w=lru_conv_w,
             lru_conv_b=lru_conv_b, lru_wa=lru_wa, lru_ba=lru_ba, lru_wx=lru_wx, lru_bx=lru_bx,
             lru_lambda=lru_lambda, w_out=w_out, ln_g=ln_g, ln_b=ln_b)
    depth = w_in.shape[0]
    alpha = (2 * depth) ** 0.25
    dec_b, dec_l, d = x_sample.shape

    rows = -(-(1 + dec_b) // SUBLANES) * SUBLANES
    cvecs = jnp.concatenate([c_ctx[None, :], c, jnp.zeros((rows - 1 - dec_b, d), jnp.float32)], axis=0)
    mod_all = _modulation(cvecs, w_mod, b_mod).reshape(depth, rows, 3, d)
    rope_tabs = _rope_tables(dec_l)

    xp, xs = x_prompt, x_sample
    rw_states, ckvs, krs, lru_states = [], [], [], []
    for l in range(depth):
        xp, (s_l, ckv_l, kr_l, h_l) = _layer(xp, mod_all[l, 0:1], _pack_layer(l, P, rope=False), alpha, None, None)
        rw_states.append(s_l)
        ckvs.append(ckv_l)
        krs.append(kr_l)
        lru_states.append(h_l)
    for l in range(depth):
        ctx = (state_rwkv[:, l], cache_mla_ckv[:, l], cache_mla_krope[:, l], state_lru[:, l])
        xs, _ = _layer(xs, mod_all[l, 1:1 + dec_b], _pack_layer(l, P, rope=True), alpha, ctx, rope_tabs)
    return (xp, xs, jnp.stack(rw_states, axis=1), jnp.stack(ckvs, axis=1), jnp.stack(krs, axis=1),
            jnp.stack(lru_states, axis=1))
```
